```python
import math
import jax, jax.numpy as jnp
from jax import lax
import numpy as np

D_MODEL = 1024
BATCH = 4
SEQ = 4096
DEPTH = 2

DSA_HEADS = 8
DSA_HEAD_DIM = 64
IDX_HEADS = 4
IDX_HEAD_DIM = 64
TOPK_MAX = 256
SB_HEADS = 8
SB_HEAD_DIM = 64
Q_BLOCK = 128
ROPE_THETA = 10000.0
LN_EPS = 1e-5
N_EXPERTS = 16
N_GROUPS = 4
EXPERTS_PER_GROUP = N_EXPERTS // N_GROUPS
TOP_K = 2
D_EXPERT = 512
DN_ALPHA = (2 * DEPTH) ** 0.25
DN_BETA = (8 * DEPTH) ** -0.25

DSA_W = DSA_HEADS * DSA_HEAD_DIM
SB_W = SB_HEADS * SB_HEAD_DIM
IN_SPLITS = (DSA_W, DSA_W, DSA_W, IDX_HEADS * IDX_HEAD_DIM, IDX_HEAD_DIM, IDX_HEADS,
             SB_W, SB_W, SB_W, 2 * D_MODEL)
IN_SCALES = (1.0, 1.0, DN_BETA, 1.0, 1.0, 1.0, 1.0, 1.0, DN_BETA, 1.0)
D_IN = sum(IN_SPLITS)
SPLIT_POINTS = tuple(int(v) for v in np.cumsum(IN_SPLITS)[:-1])

kernel_name = 'hybrid_dsa_stickbreaking_grouped_moe_deepnorm'


def layer_norm(x, g, b):
    x32 = x.astype(jnp.float32)
    mu = jnp.mean(x32, axis=-1, keepdims=True)
    var = jnp.mean(jnp.square(x32 - mu), axis=-1, keepdims=True)
    y = (x32 - mu) * lax.rsqrt(var + LN_EPS) * g.astype(jnp.float32) + b.astype(jnp.float32)
    return y.astype(x.dtype)


def rope(x):
    s, d = x.shape[1], x.shape[-1]
    half = d // 2
    inv_freq = ROPE_THETA ** (-jnp.arange(half, dtype=jnp.float32) / half)
    ang = jnp.arange(s, dtype=jnp.float32)[:, None] * inv_freq[None, :]
    cos, sin = jnp.cos(ang)[:, None, :], jnp.sin(ang)[:, None, :]
    x32 = x.astype(jnp.float32)
    x1, x2 = x32[..., :half], x32[..., half:]
    return jnp.concatenate([x1 * cos - x2 * sin, x2 * cos + x1 * sin], axis=-1).astype(x.dtype)


def to_blocks(a, nb):
    return jnp.moveaxis(a.reshape((a.shape[0], nb, Q_BLOCK) + a.shape[2:]), 1, 0)


def dsa_attention(q, k, v, q_idx, k_idx, w_idx):
    b, s = q.shape[0], q.shape[1]
    n_sel = min(TOPK_MAX, s // 4)
    nb = s // Q_BLOCK
    key_pos = jnp.arange(s)
    k_idx32 = k_idx.astype(jnp.float32)
    w_scale = IDX_HEADS ** -0.5 * IDX_HEAD_DIM ** -0.5

    def one_block(args):
        i, qb, qib, wb = args
        q_pos = i * Q_BLOCK + jnp.arange(Q_BLOCK)
        visible = key_pos[None, :] <= q_pos[:, None]
        logits = jnp.einsum('bqhd,bsd->bqhs', qib.astype(jnp.float32), k_idx32)
        index_score = jnp.einsum('bqhs,bqh->bqs', jax.nn.relu(logits),
                                 wb.astype(jnp.float32) * w_scale)
        index_score = jnp.where(visible[None], index_score, -jnp.inf)
        _, sel = lax.top_k(index_score, n_sel)
        valid = sel <= q_pos[None, :, None]
        k_sel = jax.vmap(lambda kk, ii: kk[ii])(k, sel)
        v_sel = jax.vmap(lambda vv, ii: vv[ii])(v, sel)
        scores = jnp.einsum('bqhd,bqkhd->bhqk', qb, k_sel).astype(jnp.float32) / math.sqrt(DSA_HEAD_DIM)
        scores = jnp.where(valid[:, None], scores, -jnp.inf)
        p = jax.nn.softmax(scores, axis=-1).astype(v.dtype)
        return jnp.einsum('bhqk,bqkhd->bqhd', p, v_sel)

    out = lax.map(one_block, (jnp.arange(nb), to_blocks(q, nb), to_blocks(q_idx, nb),
                              to_blocks(w_idx, nb)))
    return jnp.moveaxis(out, 0, 1).reshape(b, s, -1)


def stick_breaking_attention(q, k, v):
    b, s = q.shape[0], q.shape[1]
    nb = s // Q_BLOCK
    key_pos = jnp.arange(s)

    def one_block(args):
        i, qb = args
        q_pos = i * Q_BLOCK + jnp.arange(Q_BLOCK)
        before = key_pos[None, :] < q_pos[:, None]
        z = jnp.einsum('bqhd,bshd->bhqs', qb, k).astype(jnp.float32) / math.sqrt(SB_HEAD_DIM)
        log_beta = jax.nn.log_sigmoid(z)
        log_one_minus = jnp.where(before, jax.nn.log_sigmoid(-z), 0.0)
        later = lax.cumsum(log_one_minus, axis=3, reverse=True) - log_one_minus
        att = jnp.where(before, jnp.exp(log_beta + later), 0.0).astype(v.dtype)
        return jnp.einsum('bhqs,bshd->bqhd', att, v)

    out = lax.map(one_block, (jnp.arange(nb), to_blocks(q, nb)))
    return jnp.moveaxis(out, 0, 1).reshape(b, s, -1)


def token_mixer(x, w_in, b_gate, idx_k_norm_g, idx_k_norm_b, w_branch_a, w_branch_b, w_out):
    b, s, _ = x.shape
    proj = jnp.einsum('bsd,de->bse', x, w_in)
    q_a, k_a, v_a, q_i, k_i, w_i, q_b, k_b, v_b, gates = jnp.split(proj, SPLIT_POINTS, axis=-1)
    q_a = rope(q_a.reshape(b, s, DSA_HEADS, DSA_HEAD_DIM))
    k_a = rope(k_a.reshape(b, s, DSA_HEADS, DSA_HEAD_DIM))
    v_a = v_a.reshape(b, s, DSA_HEADS, DSA_HEAD_DIM)
    q_i = rope(q_i.reshape(b, s, IDX_HEADS, IDX_HEAD_DIM))
    k_i = rope(layer_norm(k_i, idx_k_norm_g, idx_k_norm_b)[:, :, None, :])[:, :, 0, :]
    y_a = dsa_attention(q_a, k_a, v_a, q_i, k_i, w_i)
    y_b = stick_breaking_attention(q_b.reshape(b, s, SB_HEADS, SB_HEAD_DIM),
                                   k_b.reshape(b, s, SB_HEADS, SB_HEAD_DIM),
                                   v_b.reshape(b, s, SB_HEADS, SB_HEAD_DIM))
    g = jax.nn.sigmoid((gates + b_gate).astype(jnp.float32)).astype(x.dtype)
    g_a, g_b = g[..., :D_MODEL], g[..., D_MODEL:]
    merged = g_a * jnp.einsum('bse,ed->bsd', y_a, w_branch_a) + g_b * jnp.einsum('bse,ed->bsd', y_b, w_branch_b)
    return jnp.einsum('bsd,de->bse', merged, w_out)


def grouped_moe(x, w_router, router_bias, exp_w_gate, exp_w_up, exp_w_down):
    b, s, d = x.shape
    xf = x.reshape(b * s, d)
    n = xf.shape[0]
    scores = jax.nn.sigmoid(jnp.einsum('nd,de->ne', xf, w_router).astype(jnp.float32))
    biased = scores + router_bias.astype(jnp.float32)
    group_score = lax.top_k(biased.reshape(n, N_GROUPS, EXPERTS_PER_GROUP), TOP_K)[0].sum(-1)
    g_sel = jnp.argmax(group_score, axis=-1)
    in_group = (jnp.arange(N_EXPERTS) // EXPERTS_PER_GROUP)[None, :] == g_sel[:, None]
    _, idx = lax.top_k(jnp.where(in_group, biased, -jnp.inf), TOP_K)
    w = jnp.take_along_axis(scores, idx, axis=-1)
    w = w / jnp.sum(w, axis=-1, keepdims=True)
    combine = jnp.sum(jax.nn.one_hot(idx, N_EXPERTS, dtype=jnp.float32) * w[..., None], axis=1).astype(x.dtype)
    y = jnp.zeros_like(xf)
    for e in range(N_EXPERTS):
        h = jax.nn.silu(xf @ exp_w_gate[e]) * (xf @ exp_w_up[e])
        y = y + combine[:, e:e + 1] * (h @ exp_w_down[e])
    return y.reshape(b, s, d)


def setup_inputs(seed: int = 0) -> dict:
    key = jax.random.key(seed)
    ks = jax.random.split(key, 17)
    f32 = jnp.float32
    col_scale = jnp.concatenate([jnp.full((n,), sc, f32) for n, sc in zip(IN_SPLITS, IN_SCALES)])
    nrm = lambda k, shp: jax.random.normal(k, shp, f32)
    return {
        'x': nrm(ks[0], (BATCH, SEQ, D_MODEL)),
        'w_in': nrm(ks[1], (DEPTH, D_MODEL, D_IN)) * D_MODEL ** -0.5 * col_scale,
        'b_gate': 0.1 * nrm(ks[2], (DEPTH, 2 * D_MODEL)),
        'idx_k_norm_g': 1.0 + 0.05 * nrm(ks[3], (DEPTH, IDX_HEAD_DIM)),
        'idx_k_norm_b': 0.02 * nrm(ks[4], (DEPTH, IDX_HEAD_DIM)),
        'w_branch_a': nrm(ks[5], (DEPTH, DSA_W, D_MODEL)) * DSA_W ** -0.5 * DN_BETA,
        'w_branch_b': nrm(ks[6], (DEPTH, SB_W, D_MODEL)) * SB_W ** -0.5 * DN_BETA,
        'w_out': nrm(ks[7], (DEPTH, D_MODEL, D_MODEL)) * D_MODEL ** -0.5 * DN_BETA,
        'ln1_g': 1.0 + 0.05 * nrm(ks[8], (DEPTH, D_MODEL)),
        'ln1_b': 0.02 * nrm(ks[9], (DEPTH, D_MODEL)),
        'w_router': nrm(ks[10], (D_MODEL, N_EXPERTS)) * D_MODEL ** -0.5,
        'router_bias': 0.01 * nrm(ks[11], (N_EXPERTS,)),
        'exp_w_gate': nrm(ks[12], (DEPTH, N_EXPERTS, D_MODEL, D_EXPERT)) * D_MODEL ** -0.5 * DN_BETA,
        'exp_w_up': nrm(ks[13], (DEPTH, N_EXPERTS, D_MODEL, D_EXPERT)) * D_MODEL ** -0.5 * DN_BETA,
        'exp_w_down': nrm(ks[14], (DEPTH, N_EXPERTS, D_EXPERT, D_MODEL)) * D_EXPERT ** -0.5 * DN_BETA,
        'ln2_g': 1.0 + 0.05 * nrm(ks[15], (DEPTH, D_MODEL)),
        'ln2_b': 0.02 * nrm(ks[16], (DEPTH, D_MODEL)),
    }


def reference(x, w_in, b_gate, idx_k_norm_g, idx_k_norm_b, w_branch_a, w_branch_b, w_out,
              ln1_g, ln1_b, w_router, router_bias, exp_w_gate, exp_w_up, exp_w_down,
              ln2_g, ln2_b):
    for l in range(DEPTH):
        mix = token_mixer(x, w_in[l], b_gate[l], idx_k_norm_g[l], idx_k_norm_b[l],
                          w_branch_a[l], w_branch_b[l], w_out[l])
        x = layer_norm(DN_ALPHA * x + mix, ln1_g[l], ln1_b[l])
        ffn = grouped_moe(x, w_router, router_bias, exp_w_gate[l], exp_w_up[l], exp_w_down[l])
        x = layer_norm(DN_ALPHA * x + ffn, ln2_g[l], ln2_b[l])
    return x
```

```python
import functools
import math

import jax
import jax.numpy as jnp
from jax import lax
from jax.experimental import pallas as pl
from jax.experimental.pallas import tpu as pltpu

F32 = jnp.float32
BF16 = jnp.bfloat16

C = 128
HEAD_DIM = 64
HALF = HEAD_DIM // 2
N_HEADS = 8
W_ATT = N_HEADS * HEAD_DIM
IDX_HEADS = 4
N_SEL_MAX = 256
ROPE_THETA = 10000.0
LN_EPS = 1e-5
N_EXPERTS = 16
N_GROUPS = 4
GROUP = N_EXPERTS // N_GROUPS
DEPTH = 2
DN_ALPHA = (2 * DEPTH) ** 0.25
INT_MIN = -(2 ** 31)
INT_MAX = 2 ** 31 - 1
NEG_BIG = -1e30
VMEM_LIMIT = 56 * 1024 * 1024

R_QA, R_KA, R_VA, R_QB, R_KB, R_VB = (i * W_ATT for i in range(6))
R_QI = 6 * W_ATT
R_KI = R_QI + IDX_HEADS * HEAD_DIM
R_END = R_KI + HEAD_DIM


def _dot(a, b):
    return jnp.dot(a, b, preferred_element_type=F32)


def _dot_nt(a, b):
    return lax.dot_general(a, b, (((1,), (1,)), ((), ())), preferred_element_type=F32)


def _layer_norm_rows(v, g, b):
    mu = jnp.mean(v, axis=-1, keepdims=True)
    d = v - mu
    var = jnp.mean(d * d, axis=-1, keepdims=True)
    return d * lax.rsqrt(var + LN_EPS) * g + b


def _inproj_kernel(x_ref, wt_ref, wwi_ref, cos_ref, sin_ref, g_ref, b_ref,
                   qa_ref, ka_ref, va_ref, qb_ref, kb_ref, vb_ref, qi_ref, ki_ref, wi_ref, *, tm):
    x = x_ref[0]
    xb = x.astype(BF16)
    xt = x.T.astype(BF16)
    cos = cos_ref[...]
    sin = sin_ref[...]

    def proj(r0, r1):
        return _dot(wt_ref[r0:r1, :], xt)

    def rope(p, scale):
        outs = []
        for h in range(p.shape[0] // HEAD_DIM):
            x1 = p[h * HEAD_DIM:h * HEAD_DIM + HALF]
            x2 = p[h * HEAD_DIM + HALF:(h + 1) * HEAD_DIM]
            outs.append((x1 * cos - x2 * sin) * scale)
            outs.append((x2 * cos + x1 * sin) * scale)
        return jnp.concatenate(outs, axis=0)

    def store(ref, val):
        for c in range(tm // C):
            ref[0, c] = val[:, c * C:(c + 1) * C].astype(ref.dtype)

    qscale = 1.0 / math.sqrt(HEAD_DIM)
    store(qa_ref, rope(proj(R_QA, R_KA), qscale))
    store(ka_ref, rope(proj(R_KA, R_VA), 1.0))
    store(va_ref, proj(R_VA, R_QB))
    store(qb_ref, proj(R_QB, R_KB) * qscale)
    store(kb_ref, proj(R_KB, R_VB))
    store(vb_ref, proj(R_VB, R_QI))
    store(qi_ref, rope(proj(R_QI, R_KI), 1.0))
    ki = proj(R_KI, R_END)
    mu = jnp.mean(ki, axis=0, keepdims=True)
    d = ki - mu
    var = jnp.mean(d * d, axis=0, keepdims=True)
    ki = d * lax.rsqrt(var + LN_EPS) * g_ref[...] + b_ref[...]
    store(ki_ref, rope(ki, 1.0))
    w_scale = IDX_HEADS ** -0.5 * HEAD_DIM ** -0.5
    wi_ref[0] = _dot(xb, wwi_ref[...]) * w_scale


def _inproj(x, wt, wwi, cos_t, sin_t, g, b, *, tm):
    bsz, seq, d = x.shape
    nch = seq // C
    fm = lambda rows, dt: jax.ShapeDtypeStruct((bsz, nch, rows, C), dt)
    fm_spec = lambda rows: pl.BlockSpec((1, tm // C, rows, C), lambda bi, ti: (bi, ti, 0, 0))
    full = lambda a: pl.BlockSpec(a.shape, lambda bi, ti: (0,) * a.ndim)
    return pl.pallas_call(
        functools.partial(_inproj_kernel, tm=tm),
        grid=(bsz, seq // tm),
        in_specs=[
            pl.BlockSpec((1, tm, d), lambda bi, ti: (bi, ti, 0)),
            full(wt), full(wwi),
            pl.BlockSpec((HALF, tm), lambda bi, ti: (0, ti)),
            pl.BlockSpec((HALF, tm), lambda bi, ti: (0, ti)),
            full(g), full(b),
        ],
        out_specs=[fm_spec(W_ATT)] * 6 + [fm_spec(IDX_HEADS * HEAD_DIM), fm_spec(HEAD_DIM),
                                          pl.BlockSpec((1, tm, C), lambda bi, ti: (bi, ti, 0))],
        out_shape=[fm(W_ATT, BF16)] * 6 + [fm(IDX_HEADS * HEAD_DIM, BF16), fm(HEAD_DIM, BF16),
                                           jax.ShapeDtypeStruct((bsz, seq, C), F32)],
        compiler_params=pltpu.CompilerParams(
            dimension_semantics=("parallel", "parallel"), vmem_limit_bytes=VMEM_LIMIT),
        name="inproj",
    )(x, wt, wwi, cos_t, sin_t, g, b)


def _to_natural(ref, nq):
    return jnp.concatenate([ref[0, c].astype(F32).T for c in range(nq)], axis=0)


def _dsa_kernel(qi_ref, ki_ref, wi_ref, qa_ref, ka_ref, va_ref, y_ref,
                key_s, bias_s, acc_s, m_s, l_s, *, tq, n_sel, seq):
    i = pl.program_id(1)
    nq = tq // C
    nvis = (i + 1) * nq
    rows = lax.broadcasted_iota(jnp.int32, (tq, C), 0) + i * tq
    lanes = lax.broadcasted_iota(jnp.int32, (tq, C), 1)

    qi_nat = _to_natural(qi_ref, nq)
    qi = jnp.concatenate([qi_nat[:, h * HEAD_DIM:(h + 1) * HEAD_DIM] for h in range(IDX_HEADS)],
                         axis=0).astype(BF16)
    w = wi_ref[0]

    def score_chunk(j, carry):
        lg = _dot(qi, ki_ref[0, j])
        sc = jnp.zeros((tq, C), F32)
        for h in range(IDX_HEADS):
            sc = sc + jnp.maximum(lg[h * tq:(h + 1) * tq], 0.0) * w[:, h:h + 1]
        bits = lax.bitcast_convert_type(sc + 0.0, jnp.int32)
        key = jnp.where(bits < 0, bits ^ INT_MAX, bits)
        key_s[j] = jnp.where(j * C + lanes <= rows, key, INT_MIN)
        return carry

    lax.fori_loop(0, nvis, score_chunk, 0)

    def count(pred):
        def body(j, acc):
            return acc + pred(j, key_s[j])
        acc = lax.fori_loop(0, nvis, body, jnp.zeros((tq, C), F32))
        return jnp.sum(acc, axis=-1, keepdims=True)

    def count_ge(thr):
        return count(lambda j, k: jnp.where(k >= thr, 1.0, 0.0))

    def bisect_key(_, st):
        lo, hi = st
        mid = (lo >> 1) + (hi >> 1) + (lo & hi & 1)
        ge = count_ge(mid) >= n_sel
        return jnp.where(ge, mid, lo), jnp.where(ge, hi, mid)

    lo0 = jnp.full((tq, 1), INT_MIN + 1, jnp.int32)
    hi0 = jnp.full((tq, 1), INT_MAX, jnp.int32)
    thr, _ = lax.fori_loop(0, 32, bisect_key, (lo0, hi0))

    need = n_sel - count_ge(thr + 1)

    def count_tie(mpos):
        return count(lambda j, k: jnp.where(k == thr, jnp.where(j * C + lanes <= mpos, 1.0, 0.0), 0.0))

    def bisect_pos(_, st):
        lo, hi = st
        mid = (lo + hi) >> 1
        ge = count_tie(mid) >= need
        return jnp.where(ge, lo, mid), jnp.where(ge, mid, hi)

    _, mstar = lax.fori_loop(0, max(1, (seq - 1).bit_length()), bisect_pos,
                             (jnp.full((tq, 1), -1, jnp.int32), jnp.full((tq, 1), seq - 1, jnp.int32)))

    def bias_chunk(j, carry):
        k = key_s[j]
        tie = jnp.where(k == thr, jnp.where(j * C + lanes <= mstar, 0.0, NEG_BIG), NEG_BIG)
        bias_s[j] = jnp.where(k > thr, 0.0, tie)
        return carry

    lax.fori_loop(0, nvis, bias_chunk, 0)

    qa_nat = _to_natural(qa_ref, nq)
    qh = [qa_nat[:, h * HEAD_DIM:(h + 1) * HEAD_DIM].astype(BF16) for h in range(N_HEADS)]
    m_s[...] = jnp.full(m_s.shape, NEG_BIG, F32)
    l_s[...] = jnp.zeros(l_s.shape, F32)
    acc_s[...] = jnp.zeros(acc_s.shape, F32)

    def attend(j, carry):
        bias = bias_s[j]
        for h in range(N_HEADS):
            s = _dot(qh[h], ka_ref[0, j, h * HEAD_DIM:(h + 1) * HEAD_DIM, :]) + bias
            m_old = m_s[h]
            m_new = jnp.maximum(m_old, jnp.max(s, axis=-1, keepdims=True))
            p = jnp.exp(s - m_new)
            a = jnp.exp(m_old - m_new)
            l_s[h] = a * l_s[h] + jnp.sum(p, axis=-1, keepdims=True)
            acc_s[h] = a * acc_s[h] + _dot_nt(p.astype(BF16), va_ref[0, j, h * HEAD_DIM:(h + 1) * HEAD_DIM, :])
            m_s[h] = m_new
        return carry

    lax.fori_loop(0, nvis, attend, 0)
    y_ref[0] = jnp.concatenate([acc_s[h] / l_s[h] for h in range(N_HEADS)], axis=-1).astype(y_ref.dtype)


def _dsa(qi, ki, wi, qa, ka, va, *, tq):
    bsz, nch, _, _ = qa.shape
    seq = nch * C
    n_sel = min(N_SEL_MAX, seq // 4)
    nq = tq // C
    qspec = lambda rows: pl.BlockSpec((1, nq, rows, C), lambda bi, qi_: (bi, qi_, 0, 0))
    kspec = lambda rows: pl.BlockSpec((1, nch, rows, C), lambda bi, qi_: (bi, 0, 0, 0))
    return pl.pallas_call(
        functools.partial(_dsa_kernel, tq=tq, n_sel=n_sel, seq=seq),
        grid=(bsz, seq // tq),
        in_specs=[qspec(IDX_HEADS * HEAD_DIM), kspec(HEAD_DIM),
                  pl.BlockSpec((1, tq, C), lambda bi, qi_: (bi, qi_, 0)),
                  qspec(W_ATT), kspec(W_ATT), kspec(W_ATT)],
        out_specs=pl.BlockSpec((1, tq, W_ATT), lambda bi, qi_: (bi, qi_, 0)),
        out_shape=jax.ShapeDtypeStruct((bsz, seq, W_ATT), BF16),
        scratch_shapes=[
            pltpu.VMEM((nch, tq, C), jnp.int32),
            pltpu.VMEM((nch, tq, C), F32),
            pltpu.VMEM((N_HEADS, tq, HEAD_DIM), F32),
            pltpu.VMEM((N_HEADS, tq, 1), F32),
            pltpu.VMEM((N_HEADS, tq, 1), F32),
        ],
        compiler_params=pltpu.CompilerParams(
            dimension_semantics=("parallel", "arbitrary"), vmem_limit_bytes=VMEM_LIMIT),
        name="dsa",
    )(qi, ki, wi, qa, ka, va)


def _sb_kernel(q_ref, k_ref, v_ref, u_ref, y_ref, acc_s, carry_s, *, tq):
    i = pl.program_id(1)
    nq = tq // C
    nvis = (i + 1) * nq
    rows = lax.broadcasted_iota(jnp.int32, (tq, C), 0) + i * tq
    lanes = lax.broadcasted_iota(jnp.int32, (tq, C), 1)
    q_nat = _to_natural(q_ref, nq)
    qh = [q_nat[:, h * HEAD_DIM:(h + 1) * HEAD_DIM].astype(BF16) for h in range(N_HEADS)]
    acc_s[...] = jnp.zeros(acc_s.shape, F32)
    carry_s[...] = jnp.zeros(carry_s.shape, F32)
    u = u_ref[...]

    def chunk(it, carry):
        j = nvis - 1 - it
        before = j * C + lanes < rows
        for h in range(N_HEADS):
            z = _dot(qh[h], k_ref[0, j, h * HEAD_DIM:(h + 1) * HEAD_DIM, :])
            sp = jnp.log1p(jnp.exp(-jnp.abs(z)))
            log_beta = jnp.minimum(z, 0.0) - sp
            log_rest = jnp.where(before, log_beta - z, 0.0)
            hi = log_rest.astype(BF16)
            lo = (log_rest - hi.astype(F32)).astype(BF16)
            sums = _dot(hi, u) + _dot(lo, u)
            later = carry_s[h] + sums[:, :C]
            att = jnp.where(before, jnp.exp(log_beta + later), 0.0)
            carry_s[h] = carry_s[h] + sums[:, C:]
            acc_s[h] = acc_s[h] + _dot_nt(att.astype(BF16), v_ref[0, j, h * HEAD_DIM:(h + 1) * HEAD_DIM, :])
        return carry

    lax.fori_loop(0, nvis, chunk, 0)
    y_ref[0] = jnp.concatenate([acc_s[h] for h in range(N_HEADS)], axis=-1).astype(y_ref.dtype)


def _sb(q, k, v, u, *, tq):
    bsz, nch, _, _ = q.shape
    seq = nch * C
    nq = tq // C
    return pl.pallas_call(
        functools.partial(_sb_kernel, tq=tq),
        grid=(bsz, seq // tq),
        in_specs=[pl.BlockSpec((1, nq, W_ATT, C), lambda bi, qi_: (bi, qi_, 0, 0)),
                  pl.BlockSpec((1, nch, W_ATT, C), lambda bi, qi_: (bi, 0, 0, 0)),
                  pl.BlockSpec((1, nch, W_ATT, C), lambda bi, qi_: (bi, 0, 0, 0)),
                  pl.BlockSpec(u.shape, lambda bi, qi_: (0, 0))],
        out_specs=pl.BlockSpec((1, tq, W_ATT), lambda bi, qi_: (bi, qi_, 0)),
        out_shape=jax.ShapeDtypeStruct((bsz, seq, W_ATT), BF16),
        scratch_shapes=[pltpu.VMEM((N_HEADS, tq, HEAD_DIM), F32),
                        pltpu.VMEM((N_HEADS, tq, C), F32)],
        compiler_params=pltpu.CompilerParams(
            dimension_semantics=("parallel", "arbitrary"), vmem_limit_bytes=VMEM_LIMIT),
        name="stickbreak",
    )(q, k, v, u)


def _route(scores, biased):
    neg_inf = -jnp.inf
    group_score = []
    for g in range(N_GROUPS):
        v = biased[g * GROUP:(g + 1) * GROUP]
        best = None
        for a in range(GROUP):
            for b in range(a + 1, GROUP):
                pair = v[a] + v[b]
                best = pair if best is None else jnp.maximum(best, pair)
        group_score.append(best)
    gmax = functools.reduce(jnp.maximum, group_score)
    taken = jnp.zeros_like(gmax) > 1.0
    in_group = []
    for g in range(N_GROUPS):
        sel = jnp.logical_and(group_score[g] == gmax, jnp.logical_not(taken))
        taken = jnp.logical_or(taken, sel)
        in_group.append(sel)
    masked = [jnp.where(in_group[e // GROUP], biased[e], neg_inf) for e in range(N_EXPERTS)]

    def first_argmax(vals):
        vmax = functools.reduce(jnp.maximum, vals)
        taken_ = jnp.zeros_like(vmax) > 1.0
        picks = []
        for v in vals:
            sel = jnp.logical_and(v == vmax, jnp.logical_not(taken_))
            taken_ = jnp.logical_or(taken_, sel)
            picks.append(sel)
        return picks

    pick1 = first_argmax(masked)
    masked2 = [jnp.where(pick1[e], neg_inf, masked[e]) for e in range(N_EXPERTS)]
    pick2 = first_argmax(masked2)
    w1 = functools.reduce(jnp.add, [jnp.where(pick1[e], scores[e], 0.0) for e in range(N_EXPERTS)])
    w2 = functools.reduce(jnp.add, [jnp.where(pick2[e], scores[e], 0.0) for e in range(N_EXPERTS)])
    tot = w1 + w2
    return [jnp.where(pick1[e], w1 / tot, 0.0) + jnp.where(pick2[e], w2 / tot, 0.0)
            for e in range(N_EXPERTS)]


def _split_bf16(v):
    hi = v.astype(BF16)
    return hi, (v - hi.astype(F32)).astype(BF16)


def _merge_kernel(x_ref, ya_ref, yb_ref, wa_ref, wb_ref, wg_ref, bg_ref, wo_ref, g1_ref, b1_ref,
                  wr_hi_ref, wr_lo_ref, rb_ref, x1_ref, comb_ref, *, d):
    x = x_ref[...]
    gates = jax.nn.sigmoid(_dot(x.astype(BF16), wg_ref[...]) + bg_ref[...])
    a = _dot(ya_ref[...], wa_ref[...])
    b = _dot(yb_ref[...], wb_ref[...])
    merged = gates[:, :d] * a + gates[:, d:] * b
    mix = _dot(merged.astype(BF16), wo_ref[...])
    x1 = _layer_norm_rows(DN_ALPHA * x + mix, g1_ref[...], b1_ref[...])
    x1_ref[...] = x1
    x_hi, x_lo = _split_bf16(x1)
    logits = _dot_nt(wr_hi_ref[...], x_hi) + (_dot_nt(wr_hi_ref[...], x_lo) + _dot_nt(wr_lo_ref[...], x_hi))
    sc = jax.nn.sigmoid(logits)
    bs = sc + rb_ref[...]
    scores = [sc[e:e + 1] for e in range(N_EXPERTS)]
    biased = [bs[e:e + 1] for e in range(N_EXPERTS)]
    comb_ref[...] = jnp.concatenate(_route(scores, biased), axis=0)


def _merge(x2d, ya, yb, wa, wb, wg, bg, wo, g1, b1, wr_hi, wr_lo, rb, *, tm):
    n, d = x2d.shape
    full = lambda a: pl.BlockSpec(a.shape, lambda ti: (0,) * a.ndim)
    row = lambda cols: pl.BlockSpec((tm, cols), lambda ti: (ti, 0))
    return pl.pallas_call(
        functools.partial(_merge_kernel, d=d),
        grid=(n // tm,),
        in_specs=[row(d), row(W_ATT), row(W_ATT), full(wa), full(wb), full(wg), full(bg), full(wo),
                  full(g1), full(b1), full(wr_hi), full(wr_lo), full(rb)],
        out_specs=[row(d), pl.BlockSpec((N_EXPERTS, tm), lambda ti: (0, ti))],
        out_shape=[jax.ShapeDtypeStruct((n, d), F32), jax.ShapeDtypeStruct((N_EXPERTS, n), F32)],
        compiler_params=pltpu.CompilerParams(
            dimension_semantics=("parallel",), vmem_limit_bytes=VMEM_LIMIT),
        name="merge",
    )(x2d, ya, yb, wa, wb, wg, bg, wo, g1, b1, wr_hi, wr_lo, rb)


def _moe_kernel(x_ref, comb_ref, wg_ref, wu_ref, wd_ref, g2_ref, b2_ref, out_ref, acc_s):
    e = pl.program_id(1)

    @pl.when(e == 0)
    def _():
        acc_s[...] = jnp.zeros(acc_s.shape, F32)

    xb = x_ref[...].astype(BF16)
    h = jax.nn.silu(_dot(xb, wg_ref[0])) * _dot(xb, wu_ref[0])
    lane = lax.broadcasted_iota(jnp.int32, comb_ref.shape, 1)
    c = jnp.sum(jnp.where(lane == e, comb_ref[...], 0.0), axis=-1, keepdims=True)
    acc_s[...] += c * _dot(h.astype(BF16), wd_ref[0])

    @pl.when(e == pl.num_programs(1) - 1)
    def _():
        out_ref[...] = _layer_norm_rows(DN_ALPHA * x_ref[...] + acc_s[...], g2_ref[...], b2_ref[...])


def _moe(x1, comb, wg, wu, wd, g2, b2, *, tm):
    n, d = x1.shape
    ne, _, de = wg.shape
    full = lambda a: pl.BlockSpec(a.shape, lambda ti, e: (0,) * a.ndim)
    return pl.pallas_call(
        _moe_kernel,
        grid=(n // tm, ne),
        in_specs=[pl.BlockSpec((tm, d), lambda ti, e: (ti, 0)),
                  pl.BlockSpec((tm, ne), lambda ti, e: (ti, 0)),
                  pl.BlockSpec((1, d, de), lambda ti, e: (e, 0, 0)),
                  pl.BlockSpec((1, d, de), lambda ti, e: (e, 0, 0)),
                  pl.BlockSpec((1, de, d), lambda ti, e: (e, 0, 0)),
                  full(g2), full(b2)],
        out_specs=pl.BlockSpec((tm, d), lambda ti, e: (ti, 0)),
        out_shape=jax.ShapeDtypeStruct((n, d), F32),
        scratch_shapes=[pltpu.VMEM((tm, d), F32)],
        compiler_params=pltpu.CompilerParams(
            dimension_semantics=("parallel", "arbitrary"), vmem_limit_bytes=VMEM_LIMIT),
        name="moe",
    )(x1, comb, wg, wu, wd, g2, b2)


def _rope_tables(seq):
    inv_freq = ROPE_THETA ** (-jnp.arange(HALF, dtype=F32) / HALF)
    ang = inv_freq[:, None] * jnp.arange(seq, dtype=F32)[None, :]
    return jnp.cos(ang), jnp.sin(ang)


def _later_sum_matrix():
    j = jnp.arange(C)[:, None]
    s = jnp.arange(C)[None, :]
    return jnp.concatenate([(j > s).astype(BF16), jnp.ones((C, C), BF16)], axis=1)


def kernel(x, w_in, b_gate, idx_k_norm_g, idx_k_norm_b, w_branch_a, w_branch_b, w_out, ln1_g, ln1_b,
           w_router, router_bias, exp_w_gate, exp_w_up, exp_w_down, ln2_g, ln2_b):
    bsz, seq, d = x.shape
    n = bsz * seq
    cos_t, sin_t = _rope_tables(seq)
    u = _later_sum_matrix()
    wr_hi, wr_lo = _split_bf16(w_router.T)
    rb = router_bias.reshape(N_EXPERTS, 1)
    o_qa, o_ka, o_va = 0, W_ATT, 2 * W_ATT
    o_qi = 3 * W_ATT
    o_ki = o_qi + IDX_HEADS * HEAD_DIM
    o_wi = o_ki + HEAD_DIM
    o_qb = o_wi + IDX_HEADS
    o_kb, o_vb, o_g = o_qb + W_ATT, o_qb + 2 * W_ATT, o_qb + 3 * W_ATT
    for l in range(DEPTH):
        w = w_in[l]
        wt = jnp.concatenate([w[:, o_qa:o_qi], w[:, o_qb:o_g], w[:, o_qi:o_wi]], axis=1).T.astype(BF16)
        wwi = jnp.pad(w[:, o_wi:o_qb], ((0, 0), (0, C - IDX_HEADS))).astype(BF16)
        qa, ka, va, qb, kb, vb, qi, ki, wi = _inproj(
            x, wt, wwi, cos_t, sin_t, idx_k_norm_g[l].reshape(HEAD_DIM, 1), idx_k_norm_b[l].reshape(HEAD_DIM, 1),
            tm=512)
        ya = _dsa(qi, ki, wi, qa, ka, va, tq=128)
        yb = _sb(qb, kb, vb, u, tq=128)
        x1, comb = _merge(
            x.reshape(n, d), ya.reshape(n, W_ATT), yb.reshape(n, W_ATT),
            w_branch_a[l].astype(BF16), w_branch_b[l].astype(BF16), w[:, o_g:].astype(BF16),
            b_gate[l].reshape(1, 2 * d), w_out[l].astype(BF16), ln1_g[l].reshape(1, d), ln1_b[l].reshape(1, d),
            wr_hi, wr_lo, rb, tm=512)
        x = _moe(x1, comb.T, exp_w_gate[l].astype(BF16), exp_w_up[l].astype(BF16), exp_w_down[l].astype(BF16),
                 ln2_g[l].reshape(1, d), ln2_b[l].reshape(1, d), tm=1024).reshape(bsz, seq, d)
    return x
```

```python
import functools
import math

import jax
import jax.numpy as jnp
from jax import lax
from jax.experimental import pallas as pl
from jax.experimental.pallas import tpu as pltpu

F32 = jnp.float32
BF16 = jnp.bfloat16

LANES = 128
C = 256
HEAD_DIM = 64
HALF = HEAD_DIM // 2
N_HEADS = 8
W_ATT = N_HEADS * HEAD_DIM
IDX_HEADS = 4
W_IDX = IDX_HEADS * HEAD_DIM
N_SEL_MAX = 256
ROPE_THETA = 10000.0
LN_EPS = 1e-5
N_EXPERTS = 16
N_GROUPS = 4
GROUP = N_EXPERTS // N_GROUPS
DEPTH = 2
DN_ALPHA = (2 * DEPTH) ** 0.25
INT_MIN = -(2 ** 31)
INT_MAX = 2 ** 31 - 1
NEG_BIG = -1e30
V_EXT = HEAD_DIM + 16
VMEM_LIMIT = 56 * 1024 * 1024

O_QA, O_KA, O_VA, O_QB, O_KB, O_VB = (i * W_ATT for i in range(6))
O_IDX = 6 * W_ATT
W_IDX_PAD = 384
O_END = O_IDX + W_IDX_PAD


def _dot(a, b):
    return jnp.dot(a, b, preferred_element_type=F32)


def _dot_nt(a, b):
    return lax.dot_general(a, b, (((1,), (1,)), ((), ())), preferred_element_type=F32)


def _layer_norm_rows(v, g, b):
    mu = jnp.mean(v, axis=-1, keepdims=True)
    d = v - mu
    var = jnp.mean(d * d, axis=-1, keepdims=True)
    return d * lax.rsqrt(var + LN_EPS) * g + b


def _wide(v):
    return jnp.concatenate([v] * (C // LANES), axis=-1)


def _inproj_kernel(x_ref, w_ref, wwi_ref, cos_ref, sin_ref, g_ref, b_ref,
                   qa_ref, ka_ref, va_ref, qb_ref, kb_ref, vb_ref, qi_ref, ki_ref, wi_ref, *, tm):
    xb = x_ref[0].astype(BF16)
    cos = cos_ref[...]
    sin = sin_ref[...]

    def proj_t(c0, c1):
        return _dot(xb, w_ref[:, c0:c1]).T

    def rope(p, scale):
        outs = []
        for h in range(p.shape[0] // HEAD_DIM):
            x1 = p[h * HEAD_DIM:h * HEAD_DIM + HALF]
            x2 = p[h * HEAD_DIM + HALF:(h + 1) * HEAD_DIM]
            outs.append((x1 * cos - x2 * sin) * scale)
            outs.append((x2 * cos + x1 * sin) * scale)
        return jnp.concatenate(outs, axis=0)

    def store(ref, val):
        for c in range(tm // C):
            ref[0, c] = val[:, c * C:(c + 1) * C].astype(ref.dtype)

    qscale = math.log2(math.e) / math.sqrt(HEAD_DIM)
    store(qa_ref, rope(proj_t(O_QA, O_KA), qscale))
    store(ka_ref, rope(proj_t(O_KA, O_VA), 1.0))
    store(va_ref, proj_t(O_VA, O_QB))
    store(qb_ref, proj_t(O_QB, O_KB) * qscale)
    store(kb_ref, proj_t(O_KB, O_VB))
    store(vb_ref, proj_t(O_VB, O_IDX))
    idx = proj_t(O_IDX, O_END)
    store(qi_ref, rope(idx[:W_IDX], 1.0))
    ki = idx[W_IDX:W_IDX + HEAD_DIM]
    mu = jnp.mean(ki, axis=0, keepdims=True)
    d = ki - mu
    var = jnp.mean(d * d, axis=0, keepdims=True)
    ki = d * lax.rsqrt(var + LN_EPS) * g_ref[...] + b_ref[...]
    store(ki_ref, rope(ki, 1.0))
    w_scale = IDX_HEADS ** -0.5 * HEAD_DIM ** -0.5
    wi_ref[0] = _dot(xb, wwi_ref[...]) * w_scale


def _inproj(x, w, wwi, cos_t, sin_t, g, b, *, tm):
    bsz, seq, d = x.shape
    nch = seq // C
    fm = lambda rows, dt: jax.ShapeDtypeStruct((bsz, nch, rows, C), dt)
    fm_spec = lambda rows: pl.BlockSpec((1, tm // C, rows, C), lambda bi, ti: (bi, ti, 0, 0))
    full = lambda a: pl.BlockSpec(a.shape, lambda bi, ti: (0,) * a.ndim)
    return pl.pallas_call(
        functools.partial(_inproj_kernel, tm=tm),
        grid=(bsz, seq // tm),
        in_specs=[
            pl.BlockSpec((1, tm, d), lambda bi, ti: (bi, ti, 0)),
            full(w), full(wwi),
            pl.BlockSpec((HALF, tm), lambda bi, ti: (0, ti)),
            pl.BlockSpec((HALF, tm), lambda bi, ti: (0, ti)),
            full(g), full(b),
        ],
        out_specs=[fm_spec(W_ATT)] * 6 + [fm_spec(W_IDX), fm_spec(HEAD_DIM),
                                          pl.BlockSpec((1, tm, LANES), lambda bi, ti: (bi, ti, 0))],
        out_shape=[fm(W_ATT, BF16)] * 6 + [fm(W_IDX, BF16), fm(HEAD_DIM, BF16),
                                           jax.ShapeDtypeStruct((bsz, seq, LANES), F32)],
        compiler_params=pltpu.CompilerParams(
            dimension_semantics=("parallel", "parallel"), vmem_limit_bytes=VMEM_LIMIT),
        name="inproj",
    )(x, w, wwi, cos_t, sin_t, g, b)


def _head_queries(ref, n_heads):
    nat = ref[0, 0].astype(F32).T
    return [nat[:, h * HEAD_DIM:(h + 1) * HEAD_DIM].astype(BF16) for h in range(n_heads)]


def _head_rows(h):
    return slice(h * HEAD_DIM, (h + 1) * HEAD_DIM)


def _dsa_kernel(qi_ref, ki_ref, wi_ref, qa_ref, ka_ref, va_ref, y_ref,
                key_s, bias_s, wb_s, mx_s, acc_s, *, n_sel, seq):
    i = pl.program_id(1)
    nvis = i + 1
    rows = lax.broadcasted_iota(jnp.int32, (C, C), 0)
    lanes = lax.broadcasted_iota(jnp.int32, (C, C), 1)

    qi = jnp.concatenate(_head_queries(qi_ref, IDX_HEADS), axis=0)
    w = wi_ref[0]
    for h in range(IDX_HEADS):
        wb_s[h] = jnp.broadcast_to(w[:, h:h + 1], (C, LANES))

    def score_chunk(j, carry):
        lg = _dot(qi, ki_ref[0, j])
        sc = jnp.maximum(lg[:C], 0.0) * _wide(wb_s[0])
        for h in range(1, IDX_HEADS):
            sc = sc + jnp.maximum(lg[h * C:(h + 1) * C], 0.0) * _wide(wb_s[h])
        bits = lax.bitcast_convert_type(sc + 0.0, jnp.int32)
        key = jnp.where(bits < 0, bits ^ INT_MAX, bits)
        key_s[j] = jnp.where((j - i) * C + lanes <= rows, key, INT_MIN)
        return carry

    lax.fori_loop(0, nvis, score_chunk, 0)

    def count(pred):
        def body(j, acc):
            k = key_s[j]
            for c in range(C // LANES):
                acc = acc + pred(j * C + c * LANES, k[:, c * LANES:(c + 1) * LANES])
            return acc
        acc = lax.fori_loop(0, nvis, body, jnp.zeros((C, LANES), F32))
        return jnp.sum(acc, axis=-1, keepdims=True)

    n_vis_row = (i * C + rows[:, :1] + 1).astype(F32)
    few = n_vis_row <= n_sel
    lo0 = jnp.full((C, 1), INT_MIN + 1, jnp.int32)
    hi0 = jnp.where(few, INT_MIN + 2, INT_MAX)
    n_active0 = jnp.sum(jnp.where(few, 0.0, 1.0))

    def bisect_cond(st):
        return st[4] > 0.5

    def bisect_key(st):
        lo, hi, c_lo, c_hi, _ = st
        mid = (lo >> 1) + (hi >> 1) + (lo & hi & 1)
        mid_b = jnp.broadcast_to(mid, (C, LANES))
        c = count(lambda pos, k: jnp.where(k >= mid_b, 1.0, 0.0))
        ge = c >= n_sel
        exact = c == n_sel
        lo_n = jnp.where(ge, mid, lo)
        hi_n = jnp.where(exact, mid + 1, jnp.where(ge, hi, mid))
        c_lo_n = jnp.where(ge, c, c_lo)
        c_hi_n = jnp.where(ge, c_hi, c)
        n_active = jnp.sum(jnp.where(hi_n - 1 > lo_n, 1.0, 0.0))
        return lo_n, hi_n, c_lo_n, c_hi_n, n_active

    thr, _, c_thr, c_above, _ = lax.while_loop(
        bisect_cond, bisect_key, (lo0, hi0, n_vis_row, jnp.zeros((C, 1), F32), n_active0))

    over = c_thr > n_sel
    need = jnp.where(over, n_sel - c_above, float(n_sel))
    thr_b = jnp.broadcast_to(thr, (C, LANES))
    lane1 = lax.broadcasted_iota(jnp.int32, (C, LANES), 1)

    def tie_bound(_):
        def bisect_pos(_, st):
            lo, hi = st
            mid = (lo + hi) >> 1
            mid_b = jnp.broadcast_to(mid, (C, LANES))
            c = count(lambda pos, k: jnp.where(k == thr_b, jnp.where(pos + lane1 <= mid_b, 1.0, 0.0), 0.0))
            ge = c >= need
            return jnp.where(ge, lo, mid), jnp.where(ge, mid, hi)
        _, hi = lax.fori_loop(0, max(1, (seq - 1).bit_length()), bisect_pos,
                              (jnp.full((C, 1), -1, jnp.int32), jnp.full((C, 1), seq - 1, jnp.int32)))
        return hi

    any_over = jnp.sum(jnp.where(over, 1.0, 0.0)) > 0.5
    mstar = lax.cond(any_over, tie_bound, lambda _: jnp.full((C, 1), seq - 1, jnp.int32), 0)

    thr_w = _wide(thr_b)
    mstar_w = _wide(jnp.broadcast_to(mstar, (C, LANES)))

    def bias_chunk(j, carry):
        k = key_s[j]
        tie = jnp.where(k == thr_w, jnp.where(j * C + lanes <= mstar_w, 0.0, NEG_BIG), NEG_BIG)
        bias_s[j] = jnp.where(k > thr_w, 0.0, tie)
        return carry

    lax.fori_loop(0, nvis, bias_chunk, 0)

    qh = _head_queries(qa_ref, N_HEADS)

    def scores(j):
        s = jnp.concatenate([_dot(qh[h], ka_ref[0, j, _head_rows(h), :]) for h in range(N_HEADS)], axis=0)
        return s + jnp.concatenate([bias_s[j]] * N_HEADS, axis=0)

    mx_s[...] = jnp.full(mx_s.shape, NEG_BIG, F32)

    def row_max(j, carry):
        s = scores(j)
        m = mx_s[...]
        for c in range(C // LANES):
            m = jnp.maximum(m, s[:, c * LANES:(c + 1) * LANES])
        mx_s[...] = m
        return carry

    lax.fori_loop(0, nvis, row_max, 0)
    mx_s[...] = jnp.broadcast_to(jnp.max(mx_s[...], axis=-1, keepdims=True), mx_s.shape)
    acc_s[...] = jnp.zeros(acc_s.shape, F32)
    ones_rows = jnp.ones((V_EXT - HEAD_DIM, C), BF16)

    def attend(j, carry):
        p = jnp.exp2(scores(j) - _wide(mx_s[...])).astype(BF16)
        for h in range(N_HEADS):
            v_ext = jnp.concatenate([va_ref[0, j, _head_rows(h), :], ones_rows], axis=0)
            acc_s[h] += _dot_nt(p[h * C:(h + 1) * C], v_ext)
        return carry

    lax.fori_loop(0, nvis, attend, 0)
    outs = []
    for h in range(N_HEADS):
        a = acc_s[h]
        outs.append(a[:, :HEAD_DIM] / a[:, HEAD_DIM:HEAD_DIM + 1])
    y_ref[0] = jnp.concatenate(outs, axis=-1).astype(y_ref.dtype)


def _dsa(qi, ki, wi, qa, ka, va):
    bsz, nch, _, _ = qa.shape
    seq = nch * C
    n_sel = min(N_SEL_MAX, seq // 4)
    qspec = lambda rows: pl.BlockSpec((1, 1, rows, C), lambda bi, qi_: (bi, qi_, 0, 0))
    kspec = lambda rows: pl.BlockSpec((1, nch, rows, C), lambda bi, qi_: (bi, 0, 0, 0))
    return pl.pallas_call(
        functools.partial(_dsa_kernel, n_sel=n_sel, seq=seq),
        grid=(bsz, nch),
        in_specs=[qspec(W_IDX), kspec(HEAD_DIM),
                  pl.BlockSpec((1, C, LANES), lambda bi, qi_: (bi, qi_, 0)),
                  qspec(W_ATT), kspec(W_ATT), kspec(W_ATT)],
        out_specs=pl.BlockSpec((1, C, W_ATT), lambda bi, qi_: (bi, qi_, 0)),
        out_shape=jax.ShapeDtypeStruct((bsz, seq, W_ATT), BF16),
        scratch_shapes=[
            pltpu.VMEM((nch, C, C), jnp.int32),
            pltpu.VMEM((nch, C, C), F32),
            pltpu.VMEM((IDX_HEADS, C, LANES), F32),
            pltpu.VMEM((N_HEADS * C, LANES), F32),
            pltpu.VMEM((N_HEADS, C, V_EXT), F32),
        ],
        compiler_params=pltpu.CompilerParams(
            dimension_semantics=("parallel", "arbitrary"), vmem_limit_bytes=VMEM_LIMIT),
        name="dsa",
    )(qi, ki, wi, qa, ka, va)


def _sb_kernel(q_ref, k_ref, v_ref, u_ref, y_ref, acc_s, carry_s):
    i = pl.program_id(1)
    qh = _head_queries(q_ref, N_HEADS)
    acc_s[...] = jnp.zeros(acc_s.shape, F32)
    carry_s[...] = jnp.zeros(carry_s.shape, F32)

    def step(j, diagonal):
        z = jnp.concatenate([_dot(qh[h], k_ref[0, j, _head_rows(h), :]) for h in range(N_HEADS)], axis=0)
        sp = jnp.log2(1.0 + jnp.exp2(-jnp.abs(z)))
        log_beta = jnp.minimum(z, 0.0) - sp
        log_rest = log_beta - z
        if diagonal:
            r = lax.broadcasted_iota(jnp.int32, z.shape, 0) & (C - 1)
            before = lax.broadcasted_iota(jnp.int32, z.shape, 1) < r
            log_rest = jnp.where(before, log_rest, 0.0)
        hi = log_rest.astype(BF16)
        lo = (log_rest - hi.astype(F32)).astype(BF16)
        u = u_ref[...]
        sums = _dot(hi, u) + _dot(lo, u)
        carry = carry_s[...]
        att = jnp.exp2(log_beta + (sums[:, :C] + _wide(carry)))
        if diagonal:
            att = jnp.where(before, att, 0.0)
        carry_s[...] = carry + sums[:, C:]
        att = att.astype(BF16)
        for h in range(N_HEADS):
            acc_s[h] += _dot_nt(att[h * C:(h + 1) * C], v_ref[0, j, _head_rows(h), :])

    step(i, True)

    def earlier(it, carry):
        step(i - 1 - it, False)
        return carry

    lax.fori_loop(0, i, earlier, 0)
    y_ref[0] = jnp.concatenate([acc_s[h] for h in range(N_HEADS)], axis=-1).astype(y_ref.dtype)


def _sb(q, k, v, u):
    bsz, nch, _, _ = q.shape
    seq = nch * C
    return pl.pallas_call(
        _sb_kernel,
        grid=(bsz, nch),
        in_specs=[pl.BlockSpec((1, 1, W_ATT, C), lambda bi, qi_: (bi, qi_, 0, 0)),
                  pl.BlockSpec((1, nch, W_ATT, C), lambda bi, qi_: (bi, 0, 0, 0)),
                  pl.BlockSpec((1, nch, W_ATT, C), lambda bi, qi_: (bi, 0, 0, 0)),
                  pl.BlockSpec(u.shape, lambda bi, qi_: (0, 0))],
        out_specs=pl.BlockSpec((1, C, W_ATT), lambda bi, qi_: (bi, qi_, 0)),
        out_shape=jax.ShapeDtypeStruct((bsz, seq, W_ATT), BF16),
        scratch_shapes=[pltpu.VMEM((N_HEADS, C, HEAD_DIM), F32),
                        pltpu.VMEM((N_HEADS * C, LANES), F32)],
        compiler_params=pltpu.CompilerParams(
            dimension_semantics=("parallel", "arbitrary"), vmem_limit_bytes=VMEM_LIMIT),
        name="stickbreak",
    )(q, k, v, u)


def _route(scores, biased):
    neg_inf = -jnp.inf
    group_score = []
    for g in range(N_GROUPS):
        v = biased[g * GROUP:(g + 1) * GROUP]
        best = None
        for a in range(GROUP):
            for b in range(a + 1, GROUP):
                pair = v[a] + v[b]
                best = pair if best is None else jnp.maximum(best, pair)
        group_score.append(best)
    gmax = functools.reduce(jnp.maximum, group_score)
    taken = jnp.zeros_like(gmax) > 1.0
    in_group = []
    for g in range(N_GROUPS):
        sel = jnp.logical_and(group_score[g] == gmax, jnp.logical_not(taken))
        taken = jnp.logical_or(taken, sel)
        in_group.append(sel)
    masked = [jnp.where(in_group[e // GROUP], biased[e], neg_inf) for e in range(N_EXPERTS)]

    def first_argmax(vals):
        vmax = functools.reduce(jnp.maximum, vals)
        taken_ = jnp.zeros_like(vmax) > 1.0
        picks = []
        for v in vals:
            sel = jnp.logical_and(v == vmax, jnp.logical_not(taken_))
            taken_ = jnp.logical_or(taken_, sel)
            picks.append(sel)
        return picks

    pick1 = first_argmax(masked)
    masked2 = [jnp.where(pick1[e], neg_inf, masked[e]) for e in range(N_EXPERTS)]
    pick2 = first_argmax(masked2)
    w1 = functools.reduce(jnp.add, [jnp.where(pick1[e], scores[e], 0.0) for e in range(N_EXPERTS)])
    w2 = functools.reduce(jnp.add, [jnp.where(pick2[e], scores[e], 0.0) for e in range(N_EXPERTS)])
    tot = w1 + w2
    return [jnp.where(pick1[e], w1 / tot, 0.0) + jnp.where(pick2[e], w2 / tot, 0.0)
            for e in range(N_EXPERTS)]


def _split_bf16(v):
    hi = v.astype(BF16)
    return hi, (v - hi.astype(F32)).astype(BF16)


def _merge_kernel(x_ref, ya_ref, yb_ref, wa_ref, wb_ref, wg_ref, bg_ref, wo_ref, g1_ref, b1_ref,
                  wr_hi_ref, wr_lo_ref, rb_ref, x1_ref, comb_ref, *, d):
    x = x_ref[...]
    gates = jax.nn.sigmoid(_dot(x.astype(BF16), wg_ref[...]) + bg_ref[...])
    a = _dot(ya_ref[...], wa_ref[...])
    b = _dot(yb_ref[...], wb_ref[...])
    merged = gates[:, :d] * a + gates[:, d:] * b
    mix = _dot(merged.astype(BF16), wo_ref[...])
    x1 = _layer_norm_rows(DN_ALPHA * x + mix, g1_ref[...], b1_ref[...])
    x1_ref[...] = x1
    x_hi, x_lo = _split_bf16(x1)
    logits = _dot_nt(wr_hi_ref[...], x_hi) + (_dot_nt(wr_hi_ref[...], x_lo) + _dot_nt(wr_lo_ref[...], x_hi))
    sc = jax.nn.sigmoid(logits)
    bs = sc + rb_ref[...]
    scores = [sc[e:e + 1] for e in range(N_EXPERTS)]
    biased = [bs[e:e + 1] for e in range(N_EXPERTS)]
    comb_ref[...] = jnp.concatenate(_route(scores, biased), axis=0)


def _merge(x2d, ya, yb, wa, wb, wg, bg, wo, g1, b1, wr_hi, wr_lo, rb, *, tm):
    n, d = x2d.shape
    full = lambda a: pl.BlockSpec(a.shape, lambda ti: (0,) * a.ndim)
    row = lambda cols: pl.BlockSpec((tm, cols), lambda ti: (ti, 0))
    return pl.pallas_call(
        functools.partial(_merge_kernel, d=d),
        grid=(n // tm,),
        in_specs=[row(d), row(W_ATT), row(W_ATT), full(wa), full(wb), full(wg), full(bg), full(wo),
                  full(g1), full(b1), full(wr_hi), full(wr_lo), full(rb)],
        out_specs=[row(d), pl.BlockSpec((N_EXPERTS, tm), lambda ti: (0, ti))],
        out_shape=[jax.ShapeDtypeStruct((n, d), F32), jax.ShapeDtypeStruct((N_EXPERTS, n), F32)],
        compiler_params=pltpu.CompilerParams(
            dimension_semantics=("parallel",), vmem_limit_bytes=VMEM_LIMIT),
        name="merge",
    )(x2d, ya, yb, wa, wb, wg, bg, wo, g1, b1, wr_hi, wr_lo, rb)


def _moe_kernel(x_ref, comb_ref, wg_ref, wu_ref, wd_ref, g2_ref, b2_ref, out_ref, acc_s):
    e = pl.program_id(1)

    @pl.when(e == 0)
    def _():
        acc_s[...] = jnp.zeros(acc_s.shape, F32)

    xb = x_ref[...].astype(BF16)
    h = jax.nn.silu(_dot(xb, wg_ref[0])) * _dot(xb, wu_ref[0])
    lane = lax.broadcasted_iota(jnp.int32, comb_ref.shape, 1)
    c = jnp.sum(jnp.where(lane == e, comb_ref[...], 0.0), axis=-1, keepdims=True)
    acc_s[...] += c * _dot(h.astype(BF16), wd_ref[0])

    @pl.when(e == pl.num_programs(1) - 1)
    def _():
        out_ref[...] = _layer_norm_rows(DN_ALPHA * x_ref[...] + acc_s[...], g2_ref[...], b2_ref[...])


def _moe(x1, comb, wg, wu, wd, g2, b2, *, tm):
    n, d = x1.shape
    ne, _, de = wg.shape
    full = lambda a: pl.BlockSpec(a.shape, lambda ti, e: (0,) * a.ndim)
    return pl.pallas_call(
        _moe_kernel,
        grid=(n // tm, ne),
        in_specs=[pl.BlockSpec((tm, d), lambda ti, e: (ti, 0)),
                  pl.BlockSpec((tm, ne), lambda ti, e: (ti, 0)),
                  pl.BlockSpec((1, d, de), lambda ti, e: (e, 0, 0)),
                  pl.BlockSpec((1, d, de), lambda ti, e: (e, 0, 0)),
                  pl.BlockSpec((1, de, d), lambda ti, e: (e, 0, 0)),
                  full(g2), full(b2)],
        out_specs=pl.BlockSpec((tm, d), lambda ti, e: (ti, 0)),
        out_shape=jax.ShapeDtypeStruct((n, d), F32),
        scratch_shapes=[pltpu.VMEM((tm, d), F32)],
        compiler_params=pltpu.CompilerParams(
            dimension_semantics=("parallel", "arbitrary"), vmem_limit_bytes=VMEM_LIMIT),
        name="moe",
    )(x1, comb, wg, wu, wd, g2, b2)


def _rope_tables(seq):
    inv_freq = ROPE_THETA ** (-jnp.arange(HALF, dtype=F32) / HALF)
    ang = inv_freq[:, None] * jnp.arange(seq, dtype=F32)[None, :]
    return jnp.cos(ang), jnp.sin(ang)


def _later_sum_matrix():
    j = jnp.arange(C)[:, None]
    s = jnp.arange(C)[None, :]
    return jnp.concatenate([(j > s).astype(BF16), jnp.ones((C, LANES), BF16)], axis=1)


def kernel(x, w_in, b_gate, idx_k_norm_g, idx_k_norm_b, w_branch_a, w_branch_b, w_out, ln1_g, ln1_b,
           w_router, router_bias, exp_w_gate, exp_w_up, exp_w_down, ln2_g, ln2_b):
    bsz, seq, d = x.shape
    n = bsz * seq
    cos_t, sin_t = _rope_tables(seq)
    u = _later_sum_matrix()
    wr_hi, wr_lo = _split_bf16(w_router.T)
    rb = router_bias.reshape(N_EXPERTS, 1)
    o_qi = 3 * W_ATT
    o_wi = o_qi + W_IDX + HEAD_DIM
    o_qb = o_wi + IDX_HEADS
    o_g = o_qb + 3 * W_ATT
    for l in range(DEPTH):
        w = w_in[l]
        wp = jnp.concatenate([w[:, :o_qi], w[:, o_qb:o_g], w[:, o_qi:o_wi],
                              jnp.zeros((d, W_IDX_PAD - W_IDX - HEAD_DIM), F32)], axis=1).astype(BF16)
        wwi = jnp.pad(w[:, o_wi:o_qb], ((0, 0), (0, LANES - IDX_HEADS))).astype(BF16)
        qa, ka, va, qb, kb, vb, qi, ki, wi = _inproj(
            x, wp, wwi, cos_t, sin_t, idx_k_norm_g[l].reshape(HEAD_DIM, 1), idx_k_norm_b[l].reshape(HEAD_DIM, 1),
            tm=512)
        ya = _dsa(qi, ki, wi, qa, ka, va)
        yb = _sb(qb, kb, vb, u)
        x1, comb = _merge(
            x.reshape(n, d), ya.reshape(n, W_ATT), yb.reshape(n, W_ATT),
            w_branch_a[l].astype(BF16), w_branch_b[l].astype(BF16), w[:, o_g:].astype(BF16),
            b_gate[l].reshape(1, 2 * d), w_out[l].astype(BF16), ln1_g[l].reshape(1, d), ln1_b[l].reshape(1, d),
            wr_hi, wr_lo, rb, tm=512)
        x = _moe(x1, comb.T, exp_w_gate[l].astype(BF16), exp_w_up[l].astype(BF16), exp_w_down[l].astype(BF16),
                 ln2_g[l].reshape(1, d), ln2_b[l].reshape(1, d), tm=1024).reshape(bsz, seq, d)
    return x
```

```python
import functools
import math

import jax
import jax.numpy as jnp
from jax import lax
from jax.experimental import pallas as pl
from jax.experimental.pallas import tpu as pltpu

F32 = jnp.float32
BF16 = jnp.bfloat16

LANES = 128
SUBLANES = 8
C = 256
HEAD_DIM = 64
HALF = HEAD_DIM // 2
N_HEADS = 8
N_PAIRS = N_HEADS // 2
W_ATT = N_HEADS * HEAD_DIM
IDX_HEADS = 4
W_IDX = IDX_HEADS * HEAD_DIM
N_SEL_MAX = 256
ROPE_THETA = 10000.0
LN_EPS = 1e-5
N_EXPERTS = 16
N_GROUPS = 4
GROUP = N_EXPERTS // N_GROUPS
DEPTH = 2
DN_ALPHA = (2 * DEPTH) ** 0.25
INT_MIN = -(2 ** 31)
INT_MAX = 2 ** 31 - 1
NEG_BIG = -1e30
BISECT_UNROLL = 4
V_EXT = HEAD_DIM + 16
VMEM_LIMIT = 56 * 1024 * 1024

O_QA, O_KA, O_VA, O_QB, O_KB, O_VB = (i * W_ATT for i in range(6))
O_IDX = 6 * W_ATT
W_IDX_PAD = 384
O_END = O_IDX + W_IDX_PAD


def _dot(a, b):
    return jnp.dot(a, b, preferred_element_type=F32)


def _layer_norm_rows(v, g, b):
    mu = jnp.mean(v, axis=-1, keepdims=True)
    d = v - mu
    var = jnp.mean(d * d, axis=-1, keepdims=True)
    return d * lax.rsqrt(var + LN_EPS) * g + b


def _reduce_rows(x, op):
    tiles = [x[r:r + SUBLANES] for r in range(0, x.shape[0], SUBLANES)]
    while len(tiles) > 1:
        nxt = [op(tiles[t], tiles[t + 1]) for t in range(0, len(tiles) - 1, 2)]
        if len(tiles) % 2:
            nxt.append(tiles[-1])
        tiles = nxt
    red = jnp.sum if op is jnp.add else jnp.max
    return red(tiles[0], axis=0, keepdims=True)


def _head_rows(h):
    return slice(h * HEAD_DIM, (h + 1) * HEAD_DIM)


def _pair_queries(q_t):
    zero = jnp.zeros((HEAD_DIM, C), q_t.dtype)
    blocks = []
    for p in range(N_PAIRS):
        top = jnp.concatenate([q_t[_head_rows(2 * p)], zero], axis=1)
        bot = jnp.concatenate([zero, q_t[_head_rows(2 * p + 1)]], axis=1)
        blocks.append(jnp.concatenate([top, bot], axis=0))
    return blocks


def _cast_kernel(src_ref, dst_ref):
    dst_ref[...] = src_ref[...].astype(dst_ref.dtype)


def _cast_bf16(w, *, rows):
    r, c = w.shape
    return pl.pallas_call(
        _cast_kernel,
        grid=(r // rows,),
        in_specs=[pl.BlockSpec((rows, c), lambda i: (i, 0))],
        out_specs=pl.BlockSpec((rows, c), lambda i: (i, 0)),
        out_shape=jax.ShapeDtypeStruct((r, c), BF16),
        compiler_params=pltpu.CompilerParams(dimension_semantics=("parallel",)),
        name="castw",
    )(w)


def _inproj_kernel(x_ref, w_ref, cos_ref, sin_ref, g_ref, b_ref,
                   qa_ref, ka_ref, va_ref, qb_ref, kb_ref, vb_ref, qi_ref, ki_ref, wi_ref, *, tm):
    xb = x_ref[0].astype(BF16)
    cos = cos_ref[...]
    sin = sin_ref[...]

    def proj(c0, c1):
        return _dot(xb, w_ref[:, c0:c1])

    def rope(p, scale):
        outs = []
        for h in range(p.shape[0] // HEAD_DIM):
            x1 = p[h * HEAD_DIM:h * HEAD_DIM + HALF]
            x2 = p[h * HEAD_DIM + HALF:(h + 1) * HEAD_DIM]
            outs.append((x1 * cos - x2 * sin) * scale)
            outs.append((x2 * cos + x1 * sin) * scale)
        return jnp.concatenate(outs, axis=0)

    def store_fm(ref, val):
        for c in range(tm // C):
            ref[0, c] = val[:, c * C:(c + 1) * C].astype(ref.dtype)

    qscale = math.log2(math.e) / math.sqrt(HEAD_DIM)
    store_fm(qa_ref, rope(proj(O_QA, O_KA).T, qscale))
    ka_ref[0] = rope(proj(O_KA, O_VA).T, 1.0).T.astype(ka_ref.dtype)
    store_fm(va_ref, proj(O_VA, O_QB).T)
    store_fm(qb_ref, proj(O_QB, O_KB).T * qscale)
    kb_ref[0] = proj(O_KB, O_VB).astype(kb_ref.dtype)
    store_fm(vb_ref, proj(O_VB, O_IDX).T)
    idx = proj(O_IDX, O_END).T
    store_fm(qi_ref, rope(idx[:W_IDX], 1.0))
    ki = idx[W_IDX:W_IDX + HEAD_DIM]
    mu = jnp.mean(ki, axis=0, keepdims=True)
    d = ki - mu
    var = jnp.mean(d * d, axis=0, keepdims=True)
    ki = rope(d * lax.rsqrt(var + LN_EPS) * g_ref[...] + b_ref[...], 1.0)
    ki_ref[0] = jnp.concatenate([ki, jnp.zeros_like(ki)], axis=0).T.astype(ki_ref.dtype)
    w_scale = IDX_HEADS ** -0.5 * HEAD_DIM ** -0.5
    store_fm(wi_ref, idx[W_IDX + HEAD_DIM:W_IDX + HEAD_DIM + SUBLANES] * w_scale)


def _inproj(x, w, cos_t, sin_t, g, b, *, tm):
    bsz, seq, d = x.shape
    nch = seq // C
    fm = lambda rows, dt: jax.ShapeDtypeStruct((bsz, nch, rows, C), dt)
    fm_spec = lambda rows: pl.BlockSpec((1, tm // C, rows, C), lambda bi, ti: (bi, ti, 0, 0))
    tok = lambda cols: jax.ShapeDtypeStruct((bsz, seq, cols), BF16)
    tok_spec = lambda cols: pl.BlockSpec((1, tm, cols), lambda bi, ti: (bi, ti, 0))
    full = lambda a: pl.BlockSpec(a.shape, lambda bi, ti: (0,) * a.ndim)
    return pl.pallas_call(
        functools.partial(_inproj_kernel, tm=tm),
        grid=(bsz, seq // tm),
        in_specs=[
            tok_spec(d), full(w),
            pl.BlockSpec((HALF, tm), lambda bi, ti: (0, ti)),
            pl.BlockSpec((HALF, tm), lambda bi, ti: (0, ti)),
            full(g), full(b),
        ],
        out_specs=[fm_spec(W_ATT), tok_spec(W_ATT), fm_spec(W_ATT), fm_spec(W_ATT), tok_spec(W_ATT), fm_spec(W_ATT),
                   fm_spec(W_IDX), tok_spec(LANES), fm_spec(SUBLANES)],
        out_shape=[fm(W_ATT, BF16), tok(W_ATT), fm(W_ATT, BF16), fm(W_ATT, BF16), tok(W_ATT), fm(W_ATT, BF16),
                   fm(W_IDX, BF16), tok(LANES), fm(SUBLANES, F32)],
        compiler_params=pltpu.CompilerParams(
            dimension_semantics=("parallel", "parallel"), vmem_limit_bytes=VMEM_LIMIT),
        name="inproj",
    )(x, w, cos_t, sin_t, g, b)


def _dsa_kernel(qi_ref, ki_ref, wi_ref, qa_ref, ka_ref, va_ref, y_ref,
                key_s, bias_s, m_s, acc_s, *, n_sel, seq):
    i = pl.program_id(1)
    nvis = i + 1
    krow = lax.broadcasted_iota(jnp.int32, (C, C), 0)
    qcol = lax.broadcasted_iota(jnp.int32, (C, C), 1)
    qlane = lax.broadcasted_iota(jnp.int32, (1, C), 1)

    qi_t = qi_ref[0, 0]
    qi_top = jnp.concatenate([qi_t[_head_rows(h)] for h in range(IDX_HEADS)], axis=1)
    qi_rhs = jnp.concatenate([qi_top, jnp.zeros_like(qi_top)], axis=0)
    wi = wi_ref[0, 0]

    def score_chunk(j, carry):
        lg = _dot(ki_ref[0, pl.ds(pl.multiple_of(j * C, C), C), :], qi_rhs)
        sc = jnp.maximum(lg[:, :C], 0.0) * wi[0:1]
        for h in range(1, IDX_HEADS):
            sc = sc + jnp.maximum(lg[:, h * C:(h + 1) * C], 0.0) * wi[h:h + 1]
        bits = lax.bitcast_convert_type(sc + 0.0, jnp.int32)
        key = jnp.where(bits < 0, bits ^ INT_MAX, bits)
        key_s[j] = jnp.where((j - i) * C + krow <= qcol, key, INT_MIN)
        return carry

    lax.fori_loop(0, nvis, score_chunk, 0)

    def count(pred):
        def body(j, acc):
            ind = pred(j * C, key_s[j])
            tiles = [ind[r:r + SUBLANES] for r in range(0, C, SUBLANES)]
            while len(tiles) > 1:
                tiles = [tiles[t] + tiles[t + 1] for t in range(0, len(tiles), 2)]
            return acc + tiles[0]
        acc = lax.fori_loop(0, nvis, body, jnp.zeros((SUBLANES, C), F32))
        return jnp.sum(acc, axis=0, keepdims=True)

    n_vis_q = (i * C + qlane + 1).astype(F32)
    few = n_vis_q <= n_sel
    lo0 = jnp.full((1, C), INT_MIN + 1, jnp.int32)
    hi0 = jnp.where(few, INT_MIN + 2, INT_MAX)
    n_active0 = jnp.sum(jnp.where(few, 0.0, 1.0))

    def bisect_cond(st):
        return st[4] > 0.5

    def bisect_step(st, probe=None):
        lo, hi, c_lo, c_hi, _ = st
        mid = (lo >> 1) + (hi >> 1) + (lo & hi & 1)
        if probe is not None:
            mid = jnp.where(jnp.logical_and(lo < probe, probe < hi), probe, mid)
        c = count(lambda pos, k: jnp.where(k >= mid, 1.0, 0.0))
        ge = c >= n_sel
        exact = c == n_sel
        lo_n = jnp.where(ge, mid, lo)
        hi_n = jnp.where(exact, mid + 1, jnp.where(ge, hi, mid))
        return lo_n, hi_n, jnp.where(ge, c, c_lo), jnp.where(ge, c_hi, c), st[4]

    def bisect_key(st):
        for _ in range(BISECT_UNROLL):
            st = bisect_step(st)
        lo, hi = st[0], st[1]
        return st[:4] + (jnp.sum(jnp.where(hi - 1 > lo, 1.0, 0.0)),)

    st0 = (lo0, hi0, n_vis_q, jnp.zeros((1, C), F32), n_active0)
    st0 = bisect_step(bisect_step(st0, probe=0), probe=1)
    thr, _, c_thr, c_above, _ = lax.while_loop(bisect_cond, bisect_key, st0)

    over = c_thr > n_sel
    need = jnp.where(over, n_sel - c_above, float(n_sel))

    def tie_bound(_):
        def bisect_pos(_, st):
            lo, hi = st
            mid = (lo + hi) >> 1
            c = count(lambda pos, k: jnp.where(k == thr, jnp.where(pos + krow <= mid, 1.0, 0.0), 0.0))
            ge = c >= need
            return jnp.where(ge, lo, mid), jnp.where(ge, mid, hi)
        _, hi = lax.fori_loop(0, max(1, (seq - 1).bit_length()), bisect_pos,
                              (jnp.full((1, C), -1, jnp.int32), jnp.full((1, C), seq - 1, jnp.int32)))
        return hi

    any_over = jnp.sum(jnp.where(over, 1.0, 0.0)) > 0.5
    mstar = lax.cond(any_over, tie_bound, lambda _: jnp.full((1, C), seq - 1, jnp.int32), 0)

    def bias_chunk(j, carry):
        k = key_s[j]
        tie = jnp.where(k == thr, jnp.where(j * C + krow <= mstar, 0.0, NEG_BIG), NEG_BIG)
        bias_s[j] = jnp.where(k > thr, 0.0, tie)
        return carry

    lax.fori_loop(0, nvis, bias_chunk, 0)

    q_blocks = _pair_queries(qa_ref[0, 0])
    m_s[...] = jnp.full(m_s.shape, NEG_BIG, F32)
    acc_s[...] = jnp.zeros(acc_s.shape, F32)
    ones_rows = jnp.ones((V_EXT - HEAD_DIM, C), BF16)

    def attend(j, carry):
        k_c = ka_ref[0, pl.ds(pl.multiple_of(j * C, C), C), :]
        s = jnp.concatenate([_dot(k_c[:, p * LANES:(p + 1) * LANES], q_blocks[p]) for p in range(N_PAIRS)], axis=1)
        s = s + jnp.concatenate([bias_s[j]] * N_HEADS, axis=1)
        m_old = m_s[...]
        m_new = jnp.maximum(m_old, _reduce_rows(s, jnp.maximum))
        alpha = jnp.exp2(m_old - m_new)
        pr = jnp.exp2(s - m_new).astype(BF16)
        m_s[...] = m_new
        for h in range(N_HEADS):
            v_ext = jnp.concatenate([va_ref[0, j, _head_rows(h), :], ones_rows], axis=0)
            acc_s[h] = acc_s[h] * alpha[:, h * C:(h + 1) * C] + _dot(v_ext, pr[:, h * C:(h + 1) * C])
        return carry

    lax.fori_loop(0, nvis, attend, 0)
    outs = []
    for h in range(N_HEADS):
        a = acc_s[h]
        outs.append(a[:HEAD_DIM] / a[HEAD_DIM:HEAD_DIM + 1])
    y_ref[0] = jnp.concatenate(outs, axis=0).T.astype(y_ref.dtype)


def _dsa(qi, ki, wi, qa, ka, va):
    bsz, nch, _, _ = qa.shape
    seq = nch * C
    n_sel = min(N_SEL_MAX, seq // 4)
    qspec = lambda rows: pl.BlockSpec((1, 1, rows, C), lambda bi, qi_: (bi, qi_, 0, 0))
    return pl.pallas_call(
        functools.partial(_dsa_kernel, n_sel=n_sel, seq=seq),
        grid=(bsz, nch),
        in_specs=[qspec(W_IDX),
                  pl.BlockSpec((1, seq, LANES), lambda bi, qi_: (bi, 0, 0)),
                  qspec(SUBLANES), qspec(W_ATT),
                  pl.BlockSpec((1, seq, W_ATT), lambda bi, qi_: (bi, 0, 0)),
                  pl.BlockSpec((1, nch, W_ATT, C), lambda bi, qi_: (bi, 0, 0, 0))],
        out_specs=pl.BlockSpec((1, C, W_ATT), lambda bi, qi_: (bi, qi_, 0)),
        out_shape=jax.ShapeDtypeStruct((bsz, seq, W_ATT), BF16),
        scratch_shapes=[
            pltpu.VMEM((nch, C, C), jnp.int32),
            pltpu.VMEM((nch, C, C), F32),
            pltpu.VMEM((1, N_HEADS * C), F32),
            pltpu.VMEM((N_HEADS, V_EXT, C), F32),
        ],
        compiler_params=pltpu.CompilerParams(
            dimension_semantics=("parallel", "arbitrary"), vmem_limit_bytes=VMEM_LIMIT),
        name="dsa",
    )(qi, ki, wi, qa, ka, va)


def _sb_kernel(q_ref, k_ref, v_ref, u_ref, y_ref, acc_s, carry_s):
    i = pl.program_id(1)
    q_blocks = _pair_queries(q_ref[0, 0])
    acc_s[...] = jnp.zeros(acc_s.shape, F32)
    carry_s[...] = jnp.zeros(carry_s.shape, F32)

    def step(j, diagonal):
        k_c = k_ref[0, pl.ds(pl.multiple_of(j * C, C), C), :]
        z = jnp.concatenate([_dot(k_c[:, p * LANES:(p + 1) * LANES], q_blocks[p]) for p in range(N_PAIRS)], axis=1)
        sp = jnp.log2(1.0 + jnp.exp2(-jnp.abs(z)))
        log_beta = jnp.minimum(z, 0.0) - sp
        log_rest = log_beta - z
        if diagonal:
            before = (lax.broadcasted_iota(jnp.int32, z.shape, 0)
                      < (lax.broadcasted_iota(jnp.int32, z.shape, 1) & (C - 1)))
            log_rest = jnp.where(before, log_rest, 0.0)
        hi = log_rest.astype(BF16)
        lo = (log_rest - hi.astype(F32)).astype(BF16)
        sums = _dot(u_ref[...], jnp.concatenate([hi, lo], axis=0))
        carry = carry_s[...]
        att = jnp.exp2(log_beta + (sums[:C] + carry))
        if diagonal:
            att = jnp.where(before, att, 0.0)
        carry_s[...] = carry + sums[C:C + 1]
        att = att.astype(BF16)
        for h in range(N_HEADS):
            acc_s[h] += _dot(v_ref[0, j, _head_rows(h), :], att[:, h * C:(h + 1) * C])

    step(i, True)

    def earlier(it, carry):
        step(i - 1 - it, False)
        return carry

    lax.fori_loop(0, i, earlier, 0)
    y_ref[0] = jnp.concatenate([acc_s[h] for h in range(N_HEADS)], axis=0).T.astype(y_ref.dtype)


def _sb(q, k, v, u):
    bsz, nch, _, _ = q.shape
    seq = nch * C
    return pl.pallas_call(
        _sb_kernel,
        grid=(bsz, nch),
        in_specs=[pl.BlockSpec((1, 1, W_ATT, C), lambda bi, qi_: (bi, qi_, 0, 0)),
                  pl.BlockSpec((1, seq, W_ATT), lambda bi, qi_: (bi, 0, 0)),
                  pl.BlockSpec((1, nch, W_ATT, C), lambda bi, qi_: (bi, 0, 0, 0)),
                  pl.BlockSpec(u.shape, lambda bi, qi_: (0, 0))],
        out_specs=pl.BlockSpec((1, C, W_ATT), lambda bi, qi_: (bi, qi_, 0)),
        out_shape=jax.ShapeDtypeStruct((bsz, seq, W_ATT), BF16),
        scratch_shapes=[pltpu.VMEM((N_HEADS, HEAD_DIM, C), F32),
                        pltpu.VMEM((1, N_HEADS * C), F32)],
        compiler_params=pltpu.CompilerParams(
            dimension_semantics=("parallel", "arbitrary"), vmem_limit_bytes=VMEM_LIMIT),
        name="stickbreak",
    )(q, k, v, u)


def _route(scores, biased):
    neg_inf = -jnp.inf
    group_score = []
    for g in range(N_GROUPS):
        v = biased[g * GROUP:(g + 1) * GROUP]
        best = None
        for a in range(GROUP):
            for b in range(a + 1, GROUP):
                pair = v[a] + v[b]
                best = pair if best is None else jnp.maximum(best, pair)
        group_score.append(best)
    gmax = functools.reduce(jnp.maximum, group_score)
    taken = jnp.zeros_like(gmax) > 1.0
    in_group = []
    for g in range(N_GROUPS):
        sel = jnp.logical_and(group_score[g] == gmax, jnp.logical_not(taken))
        taken = jnp.logical_or(taken, sel)
        in_group.append(sel)
    masked = [jnp.where(in_group[e // GROUP], biased[e], neg_inf) for e in range(N_EXPERTS)]

    def first_argmax(vals):
        vmax = functools.reduce(jnp.maximum, vals)
        taken_ = jnp.zeros_like(vmax) > 1.0
        picks = []
        for v in vals:
            sel = jnp.logical_and(v == vmax, jnp.logical_not(taken_))
            taken_ = jnp.logical_or(taken_, sel)
            picks.append(sel)
        return picks

    pick1 = first_argmax(masked)
    masked2 = [jnp.where(pick1[e], neg_inf, masked[e]) for e in range(N_EXPERTS)]
    pick2 = first_argmax(masked2)
    w1 = functools.reduce(jnp.add, [jnp.where(pick1[e], scores[e], 0.0) for e in range(N_EXPERTS)])
    w2 = functools.reduce(jnp.add, [jnp.where(pick2[e], scores[e], 0.0) for e in range(N_EXPERTS)])
    tot = w1 + w2
    return [jnp.where(pick1[e], w1 / tot, 0.0) + jnp.where(pick2[e], w2 / tot, 0.0)
            for e in range(N_EXPERTS)]


def _split_bf16(v):
    hi = v.astype(BF16)
    return hi, (v - hi.astype(F32)).astype(BF16)


def _merge_kernel(x_ref, ya_ref, yb_ref, wa_ref, wb_ref, wg_ref, bg_ref, wo_ref, g1_ref, b1_ref,
                  wr_hi_ref, wr_lo_ref, rb_ref, x1_ref, comb_ref, *, d):
    x = x_ref[...]
    gates = jax.nn.sigmoid(_dot(x.astype(BF16), wg_ref[...]) + bg_ref[...])
    a = _dot(ya_ref[...], wa_ref[...])
    b = _dot(yb_ref[...], wb_ref[...])
    merged = gates[:, :d] * a + gates[:, d:] * b
    mix = _dot(merged.astype(BF16), wo_ref[...])
    x1 = _layer_norm_rows(DN_ALPHA * x + mix, g1_ref[...], b1_ref[...])
    x1_ref[...] = x1
    x_hi, x_lo = _split_bf16(x1)
    nt = lambda w_, x_: lax.dot_general(w_, x_, (((1,), (1,)), ((), ())), preferred_element_type=F32)
    logits = nt(wr_hi_ref[...], x_hi) + (nt(wr_hi_ref[...], x_lo) + nt(wr_lo_ref[...], x_hi))
    sc = jax.nn.sigmoid(logits)
    bs = sc + rb_ref[...]
    scores = [sc[e:e + 1] for e in range(N_EXPERTS)]
    biased = [bs[e:e + 1] for e in range(N_EXPERTS)]
    comb_ref[...] = jnp.concatenate(_route(scores, biased), axis=0)


def _merge(x2d, ya, yb, wa, wb, wg, bg, wo, g1, b1, wr_hi, wr_lo, rb, *, tm):
    n, d = x2d.shape
    full = lambda a: pl.BlockSpec(a.shape, lambda ti: (0,) * a.ndim)
    row = lambda cols: pl.BlockSpec((tm, cols), lambda ti: (ti, 0))
    return pl.pallas_call(
        functools.partial(_merge_kernel, d=d),
        grid=(n // tm,),
        in_specs=[row(d), row(W_ATT), row(W_ATT), full(wa), full(wb), full(wg), full(bg), full(wo),
                  full(g1), full(b1), full(wr_hi), full(wr_lo), full(rb)],
        out_specs=[row(d), pl.BlockSpec((N_EXPERTS, tm), lambda ti: (0, ti))],
        out_shape=[jax.ShapeDtypeStruct((n, d), F32), jax.ShapeDtypeStruct((N_EXPERTS, n), F32)],
        compiler_params=pltpu.CompilerParams(
            dimension_semantics=("parallel",), vmem_limit_bytes=VMEM_LIMIT),
        name="merge",
    )(x2d, ya, yb, wa, wb, wg, bg, wo, g1, b1, wr_hi, wr_lo, rb)


def _moe_kernel(x_ref, comb_ref, wg_ref, wu_ref, wd_ref, g2_ref, b2_ref, out_ref, acc_s):
    e = pl.program_id(1)

    @pl.when(e == 0)
    def _():
        acc_s[...] = jnp.zeros(acc_s.shape, F32)

    xb = x_ref[...].astype(BF16)
    h = jax.nn.silu(_dot(xb, wg_ref[0])) * _dot(xb, wu_ref[0])
    lane = lax.broadcasted_iota(jnp.int32, comb_ref.shape, 1)
    c = jnp.sum(jnp.where(lane == e, comb_ref[...], 0.0), axis=-1, keepdims=True)
    acc_s[...] += c * _dot(h.astype(BF16), wd_ref[0])

    @pl.when(e == pl.num_programs(1) - 1)
    def _():
        out_ref[...] = _layer_norm_rows(DN_ALPHA * x_ref[...] + acc_s[...], g2_ref[...], b2_ref[...])


def _moe(x1, comb, wg, wu, wd, g2, b2, *, tm):
    n, d = x1.shape
    ne, _, de = wg.shape
    full = lambda a: pl.BlockSpec(a.shape, lambda ti, e: (0,) * a.ndim)
    return pl.pallas_call(
        _moe_kernel,
        grid=(n // tm, ne),
        in_specs=[pl.BlockSpec((tm, d), lambda ti, e: (ti, 0)),
                  pl.BlockSpec((tm, ne), lambda ti, e: (ti, 0)),
                  pl.BlockSpec((1, d, de), lambda ti, e: (e, 0, 0)),
                  pl.BlockSpec((1, d, de), lambda ti, e: (e, 0, 0)),
                  pl.BlockSpec((1, de, d), lambda ti, e: (e, 0, 0)),
                  full(g2), full(b2)],
        out_specs=pl.BlockSpec((tm, d), lambda ti, e: (ti, 0)),
        out_shape=jax.ShapeDtypeStruct((n, d), F32),
        scratch_shapes=[pltpu.VMEM((tm, d), F32)],
        compiler_params=pltpu.CompilerParams(
            dimension_semantics=("parallel", "arbitrary"), vmem_limit_bytes=VMEM_LIMIT),
        name="moe",
    )(x1, comb, wg, wu, wd, g2, b2)


def _rope_tables(seq):
    inv_freq = ROPE_THETA ** (-jnp.arange(HALF, dtype=F32) / HALF)
    ang = inv_freq[:, None] * jnp.arange(seq, dtype=F32)[None, :]
    return jnp.cos(ang), jnp.sin(ang)


def _later_key_matrix():
    s = jnp.arange(C)[:, None]
    j = jnp.arange(C)[None, :]
    later = (j > s).astype(BF16)
    return jnp.concatenate([jnp.concatenate([later, later], axis=1), jnp.ones((16, 2 * C), BF16)], axis=0)


def kernel(x, w_in, b_gate, idx_k_norm_g, idx_k_norm_b, w_branch_a, w_branch_b, w_out, ln1_g, ln1_b,
           w_router, router_bias, exp_w_gate, exp_w_up, exp_w_down, ln2_g, ln2_b):
    bsz, seq, d = x.shape
    n = bsz * seq
    cos_t, sin_t = _rope_tables(seq)
    u = _later_key_matrix()
    wr_hi, wr_lo = _split_bf16(w_router.T)
    rb = router_bias.reshape(N_EXPERTS, 1)
    o_qi = 3 * W_ATT
    o_qb = o_qi + W_IDX + HEAD_DIM + IDX_HEADS
    o_g = o_qb + 3 * W_ATT
    for l in range(DEPTH):
        w = w_in[l]
        wp = _cast_bf16(jnp.concatenate(
            [w[:, :o_qi], w[:, o_qb:o_g], w[:, o_qi:o_qb], jnp.zeros((d, O_END - o_g), F32)], axis=1), rows=256)
        wg = _cast_bf16(w[:, o_g:], rows=256)
        qa, ka, va, qb, kb, vb, qi, ki, wi = _inproj(
            x, wp, cos_t, sin_t, idx_k_norm_g[l].reshape(HEAD_DIM, 1), idx_k_norm_b[l].reshape(HEAD_DIM, 1), tm=512)
        ya = _dsa(qi, ki, wi, qa, ka, va)
        yb = _sb(qb, kb, vb, u)
        x1, comb = _merge(
            x.reshape(n, d), ya.reshape(n, W_ATT), yb.reshape(n, W_ATT),
            w_branch_a[l].astype(BF16), w_branch_b[l].astype(BF16), wg,
            b_gate[l].reshape(1, 2 * d), w_out[l].astype(BF16), ln1_g[l].reshape(1, d), ln1_b[l].reshape(1, d),
            wr_hi, wr_lo, rb, tm=512)
        x = _moe(x1, comb.T, exp_w_gate[l].astype(BF16), exp_w_up[l].astype(BF16), exp_w_down[l].astype(BF16),
                 ln2_g[l].reshape(1, d), ln2_b[l].reshape(1, d), tm=1024).reshape(bsz, seq, d)
    return x
```

```python
import functools
import math

import jax
import jax.numpy as jnp
from jax import lax
from jax.experimental import pallas as pl
from jax.experimental.pallas import tpu as pltpu

F32 = jnp.float32
BF16 = jnp.bfloat16

LANES = 128
SUBLANES = 8
C = 256
HEAD_DIM = 64
HALF = HEAD_DIM // 2
N_HEADS = 8
N_PAIRS = N_HEADS // 2
W_ATT = N_HEADS * HEAD_DIM
IDX_HEADS = 4
W_IDX = IDX_HEADS * HEAD_DIM
N_SEL_MAX = 256
ROPE_THETA = 10000.0
LN_EPS = 1e-5
N_EXPERTS = 16
N_GROUPS = 4
GROUP = N_EXPERTS // N_GROUPS
DEPTH = 2
DN_ALPHA = (2 * DEPTH) ** 0.25
INT_MIN = -(2 ** 31)
FLT_MIN_BITS = 0x00800000
CODE_INF = 0x7F800000 - (FLT_MIN_BITS - 1)
CODE_LOWEST = -(0x7F7FFFFF - (FLT_MIN_BITS - 1))
NEG_BIG = -1e30
BISECT_UNROLL = 4
V_EXT = HEAD_DIM + 16
VMEM_LIMIT = 56 * 1024 * 1024

O_QA, O_KA, O_VA, O_QB, O_KB, O_VB = (i * W_ATT for i in range(6))
O_IDX = 6 * W_ATT
W_IDX_PAD = 384
O_END = O_IDX + W_IDX_PAD


def _dot(a, b):
    return jnp.dot(a, b, preferred_element_type=F32)


def _layer_norm_rows(v, g, b):
    mu = jnp.mean(v, axis=-1, keepdims=True)
    d = v - mu
    var = jnp.mean(d * d, axis=-1, keepdims=True)
    return d * lax.rsqrt(var + LN_EPS) * g + b


def _reduce_rows(x, op):
    tiles = [x[r:r + SUBLANES] for r in range(0, x.shape[0], SUBLANES)]
    while len(tiles) > 1:
        nxt = [op(tiles[t], tiles[t + 1]) for t in range(0, len(tiles) - 1, 2)]
        if len(tiles) % 2:
            nxt.append(tiles[-1])
        tiles = nxt
    red = jnp.sum if op is jnp.add else jnp.max
    return red(tiles[0], axis=0, keepdims=True)


def _head_rows(h):
    return slice(h * HEAD_DIM, (h + 1) * HEAD_DIM)


def _pair_queries(q_t):
    zero = jnp.zeros((HEAD_DIM, C), q_t.dtype)
    blocks = []
    for p in range(N_PAIRS):
        top = jnp.concatenate([q_t[_head_rows(2 * p)], zero], axis=1)
        bot = jnp.concatenate([zero, q_t[_head_rows(2 * p + 1)]], axis=1)
        blocks.append(jnp.concatenate([top, bot], axis=0))
    return blocks


def _cast_kernel(src_ref, dst_ref):
    dst_ref[...] = src_ref[...].astype(dst_ref.dtype)


def _cast_bf16(w, *, rows):
    r, c = w.shape
    return pl.pallas_call(
        _cast_kernel,
        grid=(r // rows,),
        in_specs=[pl.BlockSpec((rows, c), lambda i: (i, 0))],
        out_specs=pl.BlockSpec((rows, c), lambda i: (i, 0)),
        out_shape=jax.ShapeDtypeStruct((r, c), BF16),
        compiler_params=pltpu.CompilerParams(dimension_semantics=("parallel",)),
        name="castw",
    )(w)


def _inproj_kernel(x_ref, w_ref, cos_ref, sin_ref, g_ref, b_ref,
                   qa_ref, ka_ref, va_ref, qb_ref, kb_ref, vb_ref, qi_ref, ki_ref, wi_ref, *, tm):
    xb = x_ref[0].astype(BF16)
    cos = cos_ref[...]
    sin = sin_ref[...]

    def proj(c0, c1):
        return _dot(xb, w_ref[:, c0:c1])

    def rope(p, scale):
        outs = []
        for h in range(p.shape[0] // HEAD_DIM):
            x1 = p[h * HEAD_DIM:h * HEAD_DIM + HALF]
            x2 = p[h * HEAD_DIM + HALF:(h + 1) * HEAD_DIM]
            outs.append((x1 * cos - x2 * sin) * scale)
            outs.append((x2 * cos + x1 * sin) * scale)
        return jnp.concatenate(outs, axis=0)

    def store_fm(ref, val):
        for c in range(tm // C):
            ref[0, c] = val[:, c * C:(c + 1) * C].astype(ref.dtype)

    qscale = math.log2(math.e) / math.sqrt(HEAD_DIM)
    store_fm(qa_ref, rope(proj(O_QA, O_KA).T, qscale))
    ka_ref[0] = rope(proj(O_KA, O_VA).T, 1.0).T.astype(ka_ref.dtype)
    store_fm(va_ref, proj(O_VA, O_QB).T)
    store_fm(qb_ref, proj(O_QB, O_KB).T * qscale)
    kb_ref[0] = proj(O_KB, O_VB).astype(kb_ref.dtype)
    store_fm(vb_ref, proj(O_VB, O_IDX).T)
    idx = proj(O_IDX, O_END).T
    store_fm(qi_ref, rope(idx[:W_IDX], 1.0))
    ki = idx[W_IDX:W_IDX + HEAD_DIM]
    mu = jnp.mean(ki, axis=0, keepdims=True)
    d = ki - mu
    var = jnp.mean(d * d, axis=0, keepdims=True)
    ki = rope(d * lax.rsqrt(var + LN_EPS) * g_ref[...] + b_ref[...], 1.0)
    ki_ref[0] = jnp.concatenate([ki, jnp.zeros_like(ki)], axis=0).T.astype(ki_ref.dtype)
    w_scale = IDX_HEADS ** -0.5 * HEAD_DIM ** -0.5
    store_fm(wi_ref, idx[W_IDX + HEAD_DIM:W_IDX + HEAD_DIM + SUBLANES] * w_scale)


def _inproj(x, w, cos_t, sin_t, g, b, *, tm):
    bsz, seq, d = x.shape
    nch = seq // C
    fm = lambda rows, dt: jax.ShapeDtypeStruct((bsz, nch, rows, C), dt)
    fm_spec = lambda rows: pl.BlockSpec((1, tm // C, rows, C), lambda bi, ti: (bi, ti, 0, 0))
    tok = lambda cols: jax.ShapeDtypeStruct((bsz, seq, cols), BF16)
    tok_spec = lambda cols: pl.BlockSpec((1, tm, cols), lambda bi, ti: (bi, ti, 0))
    full = lambda a: pl.BlockSpec(a.shape, lambda bi, ti: (0,) * a.ndim)
    return pl.pallas_call(
        functools.partial(_inproj_kernel, tm=tm),
        grid=(bsz, seq // tm),
        in_specs=[
            tok_spec(d), full(w),
            pl.BlockSpec((HALF, tm), lambda bi, ti: (0, ti)),
            pl.BlockSpec((HALF, tm), lambda bi, ti: (0, ti)),
            full(g), full(b),
        ],
        out_specs=[fm_spec(W_ATT), tok_spec(W_ATT), fm_spec(W_ATT), fm_spec(W_ATT), tok_spec(W_ATT), fm_spec(W_ATT),
                   fm_spec(W_IDX), tok_spec(LANES), fm_spec(SUBLANES)],
        out_shape=[fm(W_ATT, BF16), tok(W_ATT), fm(W_ATT, BF16), fm(W_ATT, BF16), tok(W_ATT), fm(W_ATT, BF16),
                   fm(W_IDX, BF16), tok(LANES), fm(SUBLANES, F32)],
        compiler_params=pltpu.CompilerParams(
            dimension_semantics=("parallel", "parallel"), vmem_limit_bytes=VMEM_LIMIT),
        name="inproj",
    )(x, w, cos_t, sin_t, g, b)


def _dsa_kernel(qi_ref, ki_ref, wi_ref, qa_ref, ka_ref, va_ref, t_ref, y_ref,
                key_s, bias_s, m_s, acc_s, *, n_sel):
    i = pl.program_id(1)
    nvis = i + 1
    krow = lax.broadcasted_iota(jnp.int32, (C, C), 0)
    qcol = lax.broadcasted_iota(jnp.int32, (C, C), 1)
    qlane = lax.broadcasted_iota(jnp.int32, (1, C), 1)

    qi_t = qi_ref[0, 0]
    qi_top = jnp.concatenate([qi_t[_head_rows(h)] for h in range(IDX_HEADS)], axis=1)
    qi_rhs = jnp.concatenate([qi_top, jnp.zeros_like(qi_top)], axis=0)
    wi = wi_ref[0, 0]

    def score_chunk(j, carry):
        lg = _dot(ki_ref[0, pl.ds(pl.multiple_of(j * C, C), C), :], qi_rhs)
        sc = jnp.maximum(lg[:, :C], 0.0) * wi[0:1]
        for h in range(1, IDX_HEADS):
            sc = sc + jnp.maximum(lg[:, h * C:(h + 1) * C], 0.0) * wi[h:h + 1]
        key_s[j] = jnp.where((j - i) * C + krow <= qcol, sc, -jnp.inf)
        return carry

    lax.fori_loop(0, nvis, score_chunk, 0)

    def count(pred):
        def body(j, acc):
            ind = pred(j * C, key_s[j])
            tiles = [ind[r:r + SUBLANES] for r in range(0, C, SUBLANES)]
            while len(tiles) > 1:
                tiles = [tiles[t] + tiles[t + 1] for t in range(0, len(tiles), 2)]
            return acc + tiles[0]
        acc = lax.fori_loop(0, nvis, body, jnp.zeros((SUBLANES, C), F32))
        return jnp.sum(acc, axis=0, keepdims=True)

    def code_to_float(code):
        mag = jnp.abs(code) + (FLT_MIN_BITS - 1)
        bits = jnp.where(code > 0, mag, jnp.where(code < 0, mag | INT_MIN, 0))
        return lax.bitcast_convert_type(bits, F32)

    n_vis_q = (i * C + qlane + 1).astype(F32)
    few = n_vis_q <= n_sel
    lo0 = jnp.full((1, C), CODE_LOWEST, jnp.int32)
    hi0 = jnp.where(few, CODE_LOWEST + 1, CODE_INF)
    n_active0 = jnp.sum(jnp.where(few, 0.0, 1.0))

    def bisect_cond(st):
        return st[4] > 0.5

    def bisect_step(st, probe=None):
        lo, hi, c_lo, c_hi, _ = st
        mid = (lo >> 1) + (hi >> 1) + (lo & hi & 1)
        if probe is not None:
            mid = jnp.where(jnp.logical_and(lo < probe, probe < hi), probe, mid)
        mid_f = code_to_float(mid)
        c = count(lambda pos, k: jnp.where(k >= mid_f, 1.0, 0.0))
        ge = c >= n_sel
        exact = c == n_sel
        lo_n = jnp.where(ge, mid, lo)
        hi_n = jnp.where(exact, mid + 1, jnp.where(ge, hi, mid))
        return lo_n, hi_n, jnp.where(ge, c, c_lo), jnp.where(ge, c_hi, c), st[4]

    def bisect_key(st):
        for _ in range(BISECT_UNROLL):
            st = bisect_step(st)
        lo, hi = st[0], st[1]
        return st[:4] + (jnp.sum(jnp.where(hi - 1 > lo, 1.0, 0.0)),)

    st0 = (lo0, hi0, n_vis_q, jnp.zeros((1, C), F32), n_active0)
    st0 = bisect_step(bisect_step(st0, probe=0), probe=1)
    thr_code, _, c_thr, c_above, _ = lax.while_loop(bisect_cond, bisect_key, st0)
    thr = code_to_float(thr_code)

    need = jnp.where(c_thr > n_sel, n_sel - c_above, float(n_sel))

    def bias_chunk(j, ties_before):
        k = key_s[j]
        tie = jnp.where(k == thr, 1.0, 0.0).astype(BF16)
        sums = _dot(t_ref[...], tie)
        keep = jnp.where(ties_before + sums[:C] < need, 0.0, NEG_BIG)
        bias_s[j] = jnp.where(k > thr, 0.0, jnp.where(k == thr, keep, NEG_BIG))
        return ties_before + sums[C:C + 1]

    lax.fori_loop(0, nvis, bias_chunk, jnp.zeros((1, C), F32))

    q_blocks = _pair_queries(qa_ref[0, 0])
    m_s[...] = jnp.full(m_s.shape, NEG_BIG, F32)
    acc_s[...] = jnp.zeros(acc_s.shape, F32)
    ones_rows = jnp.ones((V_EXT - HEAD_DIM, C), BF16)

    def attend(j, carry):
        k_c = ka_ref[0, pl.ds(pl.multiple_of(j * C, C), C), :]
        s = jnp.concatenate([_dot(k_c[:, p * LANES:(p + 1) * LANES], q_blocks[p]) for p in range(N_PAIRS)], axis=1)
        s = s + jnp.concatenate([bias_s[j]] * N_HEADS, axis=1)
        m_old = m_s[...]
        m_new = jnp.maximum(m_old, _reduce_rows(s, jnp.maximum))
        alpha = jnp.exp2(m_old - m_new)
        pr = jnp.exp2(s - m_new).astype(BF16)
        m_s[...] = m_new
        for h in range(N_HEADS):
            v_ext = jnp.concatenate([va_ref[0, j, _head_rows(h), :], ones_rows], axis=0)
            acc_s[h] = acc_s[h] * alpha[:, h * C:(h + 1) * C] + _dot(v_ext, pr[:, h * C:(h + 1) * C])
        return carry

    lax.fori_loop(0, nvis, attend, 0)
    outs = []
    for h in range(N_HEADS):
        a = acc_s[h]
        outs.append(a[:HEAD_DIM] / a[HEAD_DIM:HEAD_DIM + 1])
    y_ref[0] = jnp.concatenate(outs, axis=0).T.astype(y_ref.dtype)


def _dsa(qi, ki, wi, qa, ka, va, t):
    bsz, nch, _, _ = qa.shape
    seq = nch * C
    n_sel = min(N_SEL_MAX, seq // 4)
    qspec = lambda rows: pl.BlockSpec((1, 1, rows, C), lambda bi, qi_: (bi, qi_, 0, 0))
    return pl.pallas_call(
        functools.partial(_dsa_kernel, n_sel=n_sel),
        grid=(bsz, nch),
        in_specs=[qspec(W_IDX),
                  pl.BlockSpec((1, seq, LANES), lambda bi, qi_: (bi, 0, 0)),
                  qspec(SUBLANES), qspec(W_ATT),
                  pl.BlockSpec((1, seq, W_ATT), lambda bi, qi_: (bi, 0, 0)),
                  pl.BlockSpec((1, nch, W_ATT, C), lambda bi, qi_: (bi, 0, 0, 0)),
                  pl.BlockSpec(t.shape, lambda bi, qi_: (0, 0))],
        out_specs=pl.BlockSpec((1, C, W_ATT), lambda bi, qi_: (bi, qi_, 0)),
        out_shape=jax.ShapeDtypeStruct((bsz, seq, W_ATT), BF16),
        scratch_shapes=[
            pltpu.VMEM((nch, C, C), F32),
            pltpu.VMEM((nch, C, C), F32),
            pltpu.VMEM((1, N_HEADS * C), F32),
            pltpu.VMEM((N_HEADS, V_EXT, C), F32),
        ],
        compiler_params=pltpu.CompilerParams(
            dimension_semantics=("parallel", "arbitrary"), vmem_limit_bytes=VMEM_LIMIT),
        name="dsa",
    )(qi, ki, wi, qa, ka, va, t)


def _sb_kernel(q_ref, k_ref, v_ref, u_ref, y_ref, acc_s, carry_s):
    i = pl.program_id(1)
    q_blocks = _pair_queries(q_ref[0, 0])
    acc_s[...] = jnp.zeros(acc_s.shape, F32)
    carry_s[...] = jnp.zeros(carry_s.shape, F32)

    def step(j, diagonal):
        k_c = k_ref[0, pl.ds(pl.multiple_of(j * C, C), C), :]
        z = jnp.concatenate([_dot(k_c[:, p * LANES:(p + 1) * LANES], q_blocks[p]) for p in range(N_PAIRS)], axis=1)
        sp = jnp.log2(1.0 + jnp.exp2(-jnp.abs(z)))
        log_beta = jnp.minimum(z, 0.0) - sp
        log_rest = log_beta - z
        if diagonal:
            before = (lax.broadcasted_iota(jnp.int32, z.shape, 0)
                      < (lax.broadcasted_iota(jnp.int32, z.shape, 1) & (C - 1)))
            log_rest = jnp.where(before, log_rest, 0.0)
        sums = _dot(u_ref[...], log_rest.astype(BF16))
        carry = carry_s[...]
        att = jnp.exp2(log_beta + (sums[:C] + carry))
        if diagonal:
            att = jnp.where(before, att, 0.0)
        carry_s[...] = carry + sums[C:C + 1]
        att = att.astype(BF16)
        for h in range(N_HEADS):
            acc_s[h] += _dot(v_ref[0, j, _head_rows(h), :], att[:, h * C:(h + 1) * C])

    step(i, True)

    def earlier(it, carry):
        step(i - 1 - it, False)
        return carry

    lax.fori_loop(0, i, earlier, 0)
    y_ref[0] = jnp.concatenate([acc_s[h] for h in range(N_HEADS)], axis=0).T.astype(y_ref.dtype)


def _sb(q, k, v, u):
    bsz, nch, _, _ = q.shape
    seq = nch * C
    return pl.pallas_call(
        _sb_kernel,
        grid=(bsz, nch),
        in_specs=[pl.BlockSpec((1, 1, W_ATT, C), lambda bi, qi_: (bi, qi_, 0, 0)),
                  pl.BlockSpec((1, seq, W_ATT), lambda bi, qi_: (bi, 0, 0)),
                  pl.BlockSpec((1, nch, W_ATT, C), lambda bi, qi_: (bi, 0, 0, 0)),
                  pl.BlockSpec(u.shape, lambda bi, qi_: (0, 0))],
        out_specs=pl.BlockSpec((1, C, W_ATT), lambda bi, qi_: (bi, qi_, 0)),
        out_shape=jax.ShapeDtypeStruct((bsz, seq, W_ATT), BF16),
        scratch_shapes=[pltpu.VMEM((N_HEADS, HEAD_DIM, C), F32),
                        pltpu.VMEM((1, N_HEADS * C), F32)],
        compiler_params=pltpu.CompilerParams(
            dimension_semantics=("parallel", "arbitrary"), vmem_limit_bytes=VMEM_LIMIT),
        name="stickbreak",
    )(q, k, v, u)


def _route(scores, biased):
    neg_inf = -jnp.inf
    group_score = []
    for g in range(N_GROUPS):
        v = biased[g * GROUP:(g + 1) * GROUP]
        best = None
        for a in range(GROUP):
            for b in range(a + 1, GROUP):
                pair = v[a] + v[b]
                best = pair if best is None else jnp.maximum(best, pair)
        group_score.append(best)
    gmax = functools.reduce(jnp.maximum, group_score)
    taken = jnp.zeros_like(gmax) > 1.0
    in_group = []
    for g in range(N_GROUPS):
        sel = jnp.logical_and(group_score[g] == gmax, jnp.logical_not(taken))
        taken = jnp.logical_or(taken, sel)
        in_group.append(sel)
    masked = [jnp.where(in_group[e // GROUP], biased[e], neg_inf) for e in range(N_EXPERTS)]

    def first_argmax(vals):
        vmax = functools.reduce(jnp.maximum, vals)
        taken_ = jnp.zeros_like(vmax) > 1.0
        picks = []
        for v in vals:
            sel = jnp.logical_and(v == vmax, jnp.logical_not(taken_))
            taken_ = jnp.logical_or(taken_, sel)
            picks.append(sel)
        return picks

    pick1 = first_argmax(masked)
    masked2 = [jnp.where(pick1[e], neg_inf, masked[e]) for e in range(N_EXPERTS)]
    pick2 = first_argmax(masked2)
    w1 = functools.reduce(jnp.add, [jnp.where(pick1[e], scores[e], 0.0) for e in range(N_EXPERTS)])
    w2 = functools.reduce(jnp.add, [jnp.where(pick2[e], scores[e], 0.0) for e in range(N_EXPERTS)])
    tot = w1 + w2
    return [jnp.where(pick1[e], w1 / tot, 0.0) + jnp.where(pick2[e], w2 / tot, 0.0)
            for e in range(N_EXPERTS)]


def _split_bf16(v):
    hi = v.astype(BF16)
    return hi, (v - hi.astype(F32)).astype(BF16)


def _merge_kernel(x_ref, ya_ref, yb_ref, wa_ref, wb_ref, wg_ref, bg_ref, wo_ref, g1_ref, b1_ref,
                  wr_hi_ref, wr_lo_ref, rb_ref, x1_ref, comb_ref, *, d):
    x = x_ref[...]
    gates = jax.nn.sigmoid(_dot(x.astype(BF16), wg_ref[...]) + bg_ref[...])
    a = _dot(ya_ref[...], wa_ref[...])
    b = _dot(yb_ref[...], wb_ref[...])
    merged = gates[:, :d] * a + gates[:, d:] * b
    mix = _dot(merged.astype(BF16), wo_ref[...])
    x1 = _layer_norm_rows(DN_ALPHA * x + mix, g1_ref[...], b1_ref[...])
    x1_ref[...] = x1
    x_hi, x_lo = _split_bf16(x1)
    nt = lambda w_, x_: lax.dot_general(w_, x_, (((1,), (1,)), ((), ())), preferred_element_type=F32)
    logits = nt(wr_hi_ref[...], x_hi) + (nt(wr_hi_ref[...], x_lo) + nt(wr_lo_ref[...], x_hi))
    sc = jax.nn.sigmoid(logits)
    bs = sc + rb_ref[...]
    scores = [sc[e:e + 1] for e in range(N_EXPERTS)]
    biased = [bs[e:e + 1] for e in range(N_EXPERTS)]
    comb_ref[...] = jnp.concatenate(_route(scores, biased), axis=0)


def _merge(x2d, ya, yb, wa, wb, wg, bg, wo, g1, b1, wr_hi, wr_lo, rb, *, tm):
    n, d = x2d.shape
    full = lambda a: pl.BlockSpec(a.shape, lambda ti: (0,) * a.ndim)
    row = lambda cols: pl.BlockSpec((tm, cols), lambda ti: (ti, 0))
    return pl.pallas_call(
        functools.partial(_merge_kernel, d=d),
        grid=(n // tm,),
        in_specs=[row(d), row(W_ATT), row(W_ATT), full(wa), full(wb), full(wg), full(bg), full(wo),
                  full(g1), full(b1), full(wr_hi), full(wr_lo), full(rb)],
        out_specs=[row(d), pl.BlockSpec((N_EXPERTS, tm), lambda ti: (0, ti))],
        out_shape=[jax.ShapeDtypeStruct((n, d), F32), jax.ShapeDtypeStruct((N_EXPERTS, n), F32)],
        compiler_params=pltpu.CompilerParams(
            dimension_semantics=("parallel",), vmem_limit_bytes=VMEM_LIMIT),
        name="merge",
    )(x2d, ya, yb, wa, wb, wg, bg, wo, g1, b1, wr_hi, wr_lo, rb)


def _moe_kernel(x_ref, comb_ref, wg_ref, wu_ref, wd_ref, g2_ref, b2_ref, out_ref, acc_s):
    e = pl.program_id(1)

    @pl.when(e == 0)
    def _():
        acc_s[...] = jnp.zeros(acc_s.shape, F32)

    xb = x_ref[...].astype(BF16)
    h = jax.nn.silu(_dot(xb, wg_ref[0])) * _dot(xb, wu_ref[0])
    lane = lax.broadcasted_iota(jnp.int32, comb_ref.shape, 1)
    c = jnp.sum(jnp.where(lane == e, comb_ref[...], 0.0), axis=-1, keepdims=True)
    acc_s[...] += c * _dot(h.astype(BF16), wd_ref[0])

    @pl.when(e == pl.num_programs(1) - 1)
    def _():
        out_ref[...] = _layer_norm_rows(DN_ALPHA * x_ref[...] + acc_s[...], g2_ref[...], b2_ref[...])


def _moe(x1, comb, wg, wu, wd, g2, b2, *, tm):
    n, d = x1.shape
    ne, _, de = wg.shape
    full = lambda a: pl.BlockSpec(a.shape, lambda ti, e: (0,) * a.ndim)
    return pl.pallas_call(
        _moe_kernel,
        grid=(n // tm, ne),
        in_specs=[pl.BlockSpec((tm, d), lambda ti, e: (ti, 0)),
                  pl.BlockSpec((tm, ne), lambda ti, e: (ti, 0)),
                  pl.BlockSpec((1, d, de), lambda ti, e: (e, 0, 0)),
                  pl.BlockSpec((1, d, de), lambda ti, e: (e, 0, 0)),
                  pl.BlockSpec((1, de, d), lambda ti, e: (e, 0, 0)),
                  full(g2), full(b2)],
        out_specs=pl.BlockSpec((tm, d), lambda ti, e: (ti, 0)),
        out_shape=jax.ShapeDtypeStruct((n, d), F32),
        scratch_shapes=[pltpu.VMEM((tm, d), F32)],
        compiler_params=pltpu.CompilerParams(
            dimension_semantics=("parallel", "arbitrary"), vmem_limit_bytes=VMEM_LIMIT),
        name="moe",
    )(x1, comb, wg, wu, wd, g2, b2)


def _rope_tables(seq):
    inv_freq = ROPE_THETA ** (-jnp.arange(HALF, dtype=F32) / HALF)
    ang = inv_freq[:, None] * jnp.arange(seq, dtype=F32)[None, :]
    return jnp.cos(ang), jnp.sin(ang)


def _later_key_matrix():
    s = jnp.arange(C)[:, None]
    j = jnp.arange(C)[None, :]
    return jnp.concatenate([(j > s).astype(BF16), jnp.ones((16, C), BF16)], axis=0)


def _earlier_key_matrix():
    s = jnp.arange(C)[:, None]
    j = jnp.arange(C)[None, :]
    return jnp.concatenate([(j < s).astype(BF16), jnp.ones((16, C), BF16)], axis=0)


def kernel(x, w_in, b_gate, idx_k_norm_g, idx_k_norm_b, w_branch_a, w_branch_b, w_out, ln1_g, ln1_b,
           w_router, router_bias, exp_w_gate, exp_w_up, exp_w_down, ln2_g, ln2_b):
    bsz, seq, d = x.shape
    n = bsz * seq
    cos_t, sin_t = _rope_tables(seq)
    u = _later_key_matrix()
    t = _earlier_key_matrix()
    wr_hi, wr_lo = _split_bf16(w_router.T)
    rb = router_bias.reshape(N_EXPERTS, 1)
    o_qi = 3 * W_ATT
    o_qb = o_qi + W_IDX + HEAD_DIM + IDX_HEADS
    o_g = o_qb + 3 * W_ATT
    for l in range(DEPTH):
        w = w_in[l]
        wp = _cast_bf16(jnp.concatenate(
            [w[:, :o_qi], w[:, o_qb:o_g], w[:, o_qi:o_qb], jnp.zeros((d, O_END - o_g), F32)], axis=1), rows=256)
        wg = _cast_bf16(w[:, o_g:], rows=256)
        qa, ka, va, qb, kb, vb, qi, ki, wi = _inproj(
            x, wp, cos_t, sin_t, idx_k_norm_g[l].reshape(HEAD_DIM, 1), idx_k_norm_b[l].reshape(HEAD_DIM, 1), tm=512)
        ya = _dsa(qi, ki, wi, qa, ka, va, t)
        yb = _sb(qb, kb, vb, u)
        x1, comb = _merge(
            x.reshape(n, d), ya.reshape(n, W_ATT), yb.reshape(n, W_ATT),
            w_branch_a[l].astype(BF16), w_branch_b[l].astype(BF16), wg,
            b_gate[l].reshape(1, 2 * d), w_out[l].astype(BF16), ln1_g[l].reshape(1, d), ln1_b[l].reshape(1, d),
            wr_hi, wr_lo, rb, tm=512)
        x = _moe(x1, comb.T, exp_w_gate[l].astype(BF16), exp_w_up[l].astype(BF16), exp_w_down[l].astype(BF16),
                 ln2_g[l].reshape(1, d), ln2_b[l].reshape(1, d), tm=1024).reshape(bsz, seq, d)
    return x
```

```python
import functools
import math

import jax
import jax.numpy as jnp
from jax import lax
from jax.experimental import pallas as pl
from jax.experimental.pallas import tpu as pltpu

F32 = jnp.float32
BF16 = jnp.bfloat16

LANES = 128
SUBLANES = 8
C = 256
HEAD_DIM = 64
HALF = HEAD_DIM // 2
N_HEADS = 8
N_PAIRS = N_HEADS // 2
W_ATT = N_HEADS * HEAD_DIM
IDX_HEADS = 4
W_IDX = IDX_HEADS * HEAD_DIM
N_SEL_MAX = 256
ROPE_THETA = 10000.0
LN_EPS = 1e-5
N_EXPERTS = 16
N_GROUPS = 4
GROUP = N_EXPERTS // N_GROUPS
DEPTH = 2
DN_ALPHA = (2 * DEPTH) ** 0.25
INT_MIN = -(2 ** 31)
FLT_MIN_BITS = 0x00800000
CODE_INF = 0x7F800000 - (FLT_MIN_BITS - 1)
CODE_LOWEST = -(0x7F7FFFFF - (FLT_MIN_BITS - 1))
NEG_BIG = -1e30
MOE_TILE = 1024
MOE_SLOTS = 256
BISECT_UNROLL = 4
V_EXT = HEAD_DIM + 16
VMEM_LIMIT = 56 * 1024 * 1024

O_QA, O_KA, O_VA, O_QB, O_KB, O_VB = (i * W_ATT for i in range(6))
O_IDX = 6 * W_ATT
W_IDX_PAD = 384
O_END = O_IDX + W_IDX_PAD


def _dot(a, b):
    return jnp.dot(a, b, preferred_element_type=F32)


def _layer_norm_rows(v, g, b):
    mu = jnp.mean(v, axis=-1, keepdims=True)
    d = v - mu
    var = jnp.mean(d * d, axis=-1, keepdims=True)
    return d * lax.rsqrt(var + LN_EPS) * g + b


def _reduce_rows(x, op):
    tiles = [x[r:r + SUBLANES] for r in range(0, x.shape[0], SUBLANES)]
    while len(tiles) > 1:
        nxt = [op(tiles[t], tiles[t + 1]) for t in range(0, len(tiles) - 1, 2)]
        if len(tiles) % 2:
            nxt.append(tiles[-1])
        tiles = nxt
    red = jnp.sum if op is jnp.add else jnp.max
    return red(tiles[0], axis=0, keepdims=True)


def _head_rows(h):
    return slice(h * HEAD_DIM, (h + 1) * HEAD_DIM)


def _pair_queries(q_t):
    zero = jnp.zeros((HEAD_DIM, C), q_t.dtype)
    blocks = []
    for p in range(N_PAIRS):
        top = jnp.concatenate([q_t[_head_rows(2 * p)], zero], axis=1)
        bot = jnp.concatenate([zero, q_t[_head_rows(2 * p + 1)]], axis=1)
        blocks.append(jnp.concatenate([top, bot], axis=0))
    return blocks


def _cast_kernel(src_ref, dst_ref):
    dst_ref[...] = src_ref[...].astype(dst_ref.dtype)


def _cast_bf16(w, *, rows):
    r, c = w.shape
    return pl.pallas_call(
        _cast_kernel,
        grid=(r // rows,),
        in_specs=[pl.BlockSpec((rows, c), lambda i: (i, 0))],
        out_specs=pl.BlockSpec((rows, c), lambda i: (i, 0)),
        out_shape=jax.ShapeDtypeStruct((r, c), BF16),
        compiler_params=pltpu.CompilerParams(dimension_semantics=("parallel",)),
        name="castw",
    )(w)


def _inproj_kernel(x_ref, w_ref, cos_ref, sin_ref, g_ref, b_ref,
                   qa_ref, ka_ref, va_ref, qb_ref, kb_ref, vb_ref, qi_ref, ki_ref, wi_ref, *, tm):
    xb = x_ref[0].astype(BF16)
    cos = cos_ref[...]
    sin = sin_ref[...]

    def proj(c0, c1):
        return _dot(xb, w_ref[:, c0:c1])

    def rope(p, scale):
        outs = []
        for h in range(p.shape[0] // HEAD_DIM):
            x1 = p[h * HEAD_DIM:h * HEAD_DIM + HALF]
            x2 = p[h * HEAD_DIM + HALF:(h + 1) * HEAD_DIM]
            outs.append((x1 * cos - x2 * sin) * scale)
            outs.append((x2 * cos + x1 * sin) * scale)
        return jnp.concatenate(outs, axis=0)

    def store_fm(ref, val):
        for c in range(tm // C):
            ref[0, c] = val[:, c * C:(c + 1) * C].astype(ref.dtype)

    qscale = math.log2(math.e) / math.sqrt(HEAD_DIM)
    store_fm(qa_ref, rope(proj(O_QA, O_KA).T, qscale))
    ka_ref[0] = rope(proj(O_KA, O_VA).T, 1.0).T.astype(ka_ref.dtype)
    store_fm(va_ref, proj(O_VA, O_QB).T)
    store_fm(qb_ref, proj(O_QB, O_KB).T * qscale)
    kb_ref[0] = proj(O_KB, O_VB).astype(kb_ref.dtype)
    store_fm(vb_ref, proj(O_VB, O_IDX).T)
    idx = proj(O_IDX, O_END).T
    store_fm(qi_ref, rope(idx[:W_IDX], 1.0))
    ki = idx[W_IDX:W_IDX + HEAD_DIM]
    mu = jnp.mean(ki, axis=0, keepdims=True)
    d = ki - mu
    var = jnp.mean(d * d, axis=0, keepdims=True)
    ki = rope(d * lax.rsqrt(var + LN_EPS) * g_ref[...] + b_ref[...], 1.0)
    ki_ref[0] = jnp.concatenate([ki, jnp.zeros_like(ki)], axis=0).T.astype(ki_ref.dtype)
    w_scale = IDX_HEADS ** -0.5 * HEAD_DIM ** -0.5
    store_fm(wi_ref, idx[W_IDX + HEAD_DIM:W_IDX + HEAD_DIM + SUBLANES] * w_scale)


def _inproj(x, w, cos_t, sin_t, g, b, *, tm):
    bsz, seq, d = x.shape
    nch = seq // C
    fm = lambda rows, dt: jax.ShapeDtypeStruct((bsz, nch, rows, C), dt)
    fm_spec = lambda rows: pl.BlockSpec((1, tm // C, rows, C), lambda bi, ti: (bi, ti, 0, 0))
    tok = lambda cols: jax.ShapeDtypeStruct((bsz, seq, cols), BF16)
    tok_spec = lambda cols: pl.BlockSpec((1, tm, cols), lambda bi, ti: (bi, ti, 0))
    full = lambda a: pl.BlockSpec(a.shape, lambda bi, ti: (0,) * a.ndim)
    return pl.pallas_call(
        functools.partial(_inproj_kernel, tm=tm),
        grid=(bsz, seq // tm),
        in_specs=[
            tok_spec(d), full(w),
            pl.BlockSpec((HALF, tm), lambda bi, ti: (0, ti)),
            pl.BlockSpec((HALF, tm), lambda bi, ti: (0, ti)),
            full(g), full(b),
        ],
        out_specs=[fm_spec(W_ATT), tok_spec(W_ATT), fm_spec(W_ATT), fm_spec(W_ATT), tok_spec(W_ATT), fm_spec(W_ATT),
                   fm_spec(W_IDX), tok_spec(LANES), fm_spec(SUBLANES)],
        out_shape=[fm(W_ATT, BF16), tok(W_ATT), fm(W_ATT, BF16), fm(W_ATT, BF16), tok(W_ATT), fm(W_ATT, BF16),
                   fm(W_IDX, BF16), tok(LANES), fm(SUBLANES, F32)],
        compiler_params=pltpu.CompilerParams(
            dimension_semantics=("parallel", "parallel"), vmem_limit_bytes=VMEM_LIMIT),
        name="inproj",
    )(x, w, cos_t, sin_t, g, b)


def _dsa_kernel(qi_ref, ki_ref, wi_ref, qa_ref, ka_ref, va_ref, t_ref, y_ref,
                key_s, bias_s, m_s, acc_s, *, n_sel):
    i = pl.program_id(1)
    nvis = i + 1
    krow = lax.broadcasted_iota(jnp.int32, (C, C), 0)
    qcol = lax.broadcasted_iota(jnp.int32, (C, C), 1)
    qlane = lax.broadcasted_iota(jnp.int32, (1, C), 1)

    qi_t = qi_ref[0, 0]
    qi_top = jnp.concatenate([qi_t[_head_rows(h)] for h in range(IDX_HEADS)], axis=1)
    qi_rhs = jnp.concatenate([qi_top, jnp.zeros_like(qi_top)], axis=0)
    wi = wi_ref[0, 0]

    def score_chunk(j, carry):
        lg = _dot(ki_ref[0, pl.ds(pl.multiple_of(j * C, C), C), :], qi_rhs)
        sc = jnp.maximum(lg[:, :C], 0.0) * wi[0:1]
        for h in range(1, IDX_HEADS):
            sc = sc + jnp.maximum(lg[:, h * C:(h + 1) * C], 0.0) * wi[h:h + 1]
        key_s[j] = jnp.where((j - i) * C + krow <= qcol, sc, -jnp.inf)
        return carry

    lax.fori_loop(0, nvis, score_chunk, 0)

    def count(pred):
        def body(j, acc):
            ind = pred(j * C, key_s[j])
            tiles = [ind[r:r + SUBLANES] for r in range(0, C, SUBLANES)]
            while len(tiles) > 1:
                tiles = [tiles[t] + tiles[t + 1] for t in range(0, len(tiles), 2)]
            return acc + tiles[0]
        acc = lax.fori_loop(0, nvis, body, jnp.zeros((SUBLANES, C), F32))
        return jnp.sum(acc, axis=0, keepdims=True)

    def code_to_float(code):
        mag = jnp.abs(code) + (FLT_MIN_BITS - 1)
        bits = jnp.where(code > 0, mag, jnp.where(code < 0, mag | INT_MIN, 0))
        return lax.bitcast_convert_type(bits, F32)

    n_vis_q = (i * C + qlane + 1).astype(F32)
    few = n_vis_q <= n_sel
    lo0 = jnp.full((1, C), CODE_LOWEST, jnp.int32)
    hi0 = jnp.where(few, CODE_LOWEST + 1, CODE_INF)
    n_active0 = jnp.sum(jnp.where(few, 0.0, 1.0))

    def bisect_cond(st):
        return st[4] > 0.5

    def bisect_step(st, probe=None):
        lo, hi, c_lo, c_hi, _ = st
        mid = (lo >> 1) + (hi >> 1) + (lo & hi & 1)
        if probe is not None:
            mid = jnp.where(jnp.logical_and(lo < probe, probe < hi), probe, mid)
        mid_f = code_to_float(mid)
        c = count(lambda pos, k: jnp.where(k >= mid_f, 1.0, 0.0))
        ge = c >= n_sel
        exact = c == n_sel
        lo_n = jnp.where(ge, mid, lo)
        hi_n = jnp.where(exact, mid + 1, jnp.where(ge, hi, mid))
        return lo_n, hi_n, jnp.where(ge, c, c_lo), jnp.where(ge, c_hi, c), st[4]

    def bisect_key(st):
        for _ in range(BISECT_UNROLL):
            st = bisect_step(st)
        lo, hi = st[0], st[1]
        return st[:4] + (jnp.sum(jnp.where(hi - 1 > lo, 1.0, 0.0)),)

    st0 = (lo0, hi0, n_vis_q, jnp.zeros((1, C), F32), n_active0)
    st0 = bisect_step(bisect_step(st0, probe=0), probe=1)
    thr_code, _, c_thr, c_above, _ = lax.while_loop(bisect_cond, bisect_key, st0)
    thr = code_to_float(thr_code)

    need = jnp.where(c_thr > n_sel, n_sel - c_above, float(n_sel))

    def bias_chunk(j, ties_before):
        k = key_s[j]
        tie = jnp.where(k == thr, 1.0, 0.0).astype(BF16)
        sums = _dot(t_ref[...], tie)
        keep = jnp.where(ties_before + sums[:C] < need, 0.0, NEG_BIG)
        bias_s[j] = jnp.where(k > thr, 0.0, jnp.where(k == thr, keep, NEG_BIG))
        return ties_before + sums[C:C + 1]

    lax.fori_loop(0, nvis, bias_chunk, jnp.zeros((1, C), F32))

    q_blocks = _pair_queries(qa_ref[0, 0])
    m_s[...] = jnp.full(m_s.shape, NEG_BIG, F32)
    acc_s[...] = jnp.zeros(acc_s.shape, F32)
    ones_rows = jnp.ones((V_EXT - HEAD_DIM, C), BF16)

    def attend(j, carry):
        k_c = ka_ref[0, pl.ds(pl.multiple_of(j * C, C), C), :]
        s = jnp.concatenate([_dot(k_c[:, p * LANES:(p + 1) * LANES], q_blocks[p]) for p in range(N_PAIRS)], axis=1)
        s = s + jnp.concatenate([bias_s[j]] * N_HEADS, axis=1)
        m_old = m_s[...]
        m_new = jnp.maximum(m_old, _reduce_rows(s, jnp.maximum))
        alpha = jnp.exp2(m_old - m_new)
        pr = jnp.exp2(s - m_new).astype(BF16)
        m_s[...] = m_new
        for h in range(N_HEADS):
            v_ext = jnp.concatenate([va_ref[0, j, _head_rows(h), :], ones_rows], axis=0)
            acc_s[h] = acc_s[h] * alpha[:, h * C:(h + 1) * C] + _dot(v_ext, pr[:, h * C:(h + 1) * C])
        return carry

    lax.fori_loop(0, nvis, attend, 0)
    outs = []
    for h in range(N_HEADS):
        a = acc_s[h]
        outs.append(a[:HEAD_DIM] / a[HEAD_DIM:HEAD_DIM + 1])
    y_ref[0] = jnp.concatenate(outs, axis=0).T.astype(y_ref.dtype)


def _dsa(qi, ki, wi, qa, ka, va, t):
    bsz, nch, _, _ = qa.shape
    seq = nch * C
    n_sel = min(N_SEL_MAX, seq // 4)
    qspec = lambda rows: pl.BlockSpec((1, 1, rows, C), lambda bi, qi_: (bi, qi_, 0, 0))
    return pl.pallas_call(
        functools.partial(_dsa_kernel, n_sel=n_sel),
        grid=(bsz, nch),
        in_specs=[qspec(W_IDX),
                  pl.BlockSpec((1, seq, LANES), lambda bi, qi_: (bi, 0, 0)),
                  qspec(SUBLANES), qspec(W_ATT),
                  pl.BlockSpec((1, seq, W_ATT), lambda bi, qi_: (bi, 0, 0)),
                  pl.BlockSpec((1, nch, W_ATT, C), lambda bi, qi_: (bi, 0, 0, 0)),
                  pl.BlockSpec(t.shape, lambda bi, qi_: (0, 0))],
        out_specs=pl.BlockSpec((1, C, W_ATT), lambda bi, qi_: (bi, qi_, 0)),
        out_shape=jax.ShapeDtypeStruct((bsz, seq, W_ATT), BF16),
        scratch_shapes=[
            pltpu.VMEM((nch, C, C), F32),
            pltpu.VMEM((nch, C, C), F32),
            pltpu.VMEM((1, N_HEADS * C), F32),
            pltpu.VMEM((N_HEADS, V_EXT, C), F32),
        ],
        compiler_params=pltpu.CompilerParams(
            dimension_semantics=("parallel", "arbitrary"), vmem_limit_bytes=VMEM_LIMIT),
        name="dsa",
    )(qi, ki, wi, qa, ka, va, t)


def _sb_kernel(q_ref, k_ref, v_ref, u_ref, y_ref, acc_s, carry_s):
    i = pl.program_id(1)
    q_blocks = _pair_queries(q_ref[0, 0])
    acc_s[...] = jnp.zeros(acc_s.shape, F32)
    carry_s[...] = jnp.zeros(carry_s.shape, F32)

    def step(j, diagonal):
        k_c = k_ref[0, pl.ds(pl.multiple_of(j * C, C), C), :]
        z = jnp.concatenate([_dot(k_c[:, p * LANES:(p + 1) * LANES], q_blocks[p]) for p in range(N_PAIRS)], axis=1)
        sp = jnp.log2(1.0 + jnp.exp2(-jnp.abs(z)))
        log_beta = jnp.minimum(z, 0.0) - sp
        log_rest = log_beta - z
        if diagonal:
            before = (lax.broadcasted_iota(jnp.int32, z.shape, 0)
                      < (lax.broadcasted_iota(jnp.int32, z.shape, 1) & (C - 1)))
            log_rest = jnp.where(before, log_rest, 0.0)
        sums = _dot(u_ref[...], log_rest.astype(BF16))
        carry = carry_s[...]
        att = jnp.exp2(log_beta + (sums[:C] + carry))
        if diagonal:
            att = jnp.where(before, att, 0.0)
        carry_s[...] = carry + sums[C:C + 1]
        att = att.astype(BF16)
        for h in range(N_HEADS):
            acc_s[h] += _dot(v_ref[0, j, _head_rows(h), :], att[:, h * C:(h + 1) * C])

    step(i, True)

    def earlier(it, carry):
        step(i - 1 - it, False)
        return carry

    lax.fori_loop(0, i, earlier, 0)
    y_ref[0] = jnp.concatenate([acc_s[h] for h in range(N_HEADS)], axis=0).T.astype(y_ref.dtype)


def _sb(q, k, v, u):
    bsz, nch, _, _ = q.shape
    seq = nch * C
    return pl.pallas_call(
        _sb_kernel,
        grid=(bsz, nch),
        in_specs=[pl.BlockSpec((1, 1, W_ATT, C), lambda bi, qi_: (bi, qi_, 0, 0)),
                  pl.BlockSpec((1, seq, W_ATT), lambda bi, qi_: (bi, 0, 0)),
                  pl.BlockSpec((1, nch, W_ATT, C), lambda bi, qi_: (bi, 0, 0, 0)),
                  pl.BlockSpec(u.shape, lambda bi, qi_: (0, 0))],
        out_specs=pl.BlockSpec((1, C, W_ATT), lambda bi, qi_: (bi, qi_, 0)),
        out_shape=jax.ShapeDtypeStruct((bsz, seq, W_ATT), BF16),
        scratch_shapes=[pltpu.VMEM((N_HEADS, HEAD_DIM, C), F32),
                        pltpu.VMEM((1, N_HEADS * C), F32)],
        compiler_params=pltpu.CompilerParams(
            dimension_semantics=("parallel", "arbitrary"), vmem_limit_bytes=VMEM_LIMIT),
        name="stickbreak",
    )(q, k, v, u)


def _route(scores, biased):
    neg_inf = -jnp.inf
    group_score = []
    for g in range(N_GROUPS):
        v = biased[g * GROUP:(g + 1) * GROUP]
        best = None
        for a in range(GROUP):
            for b in range(a + 1, GROUP):
                pair = v[a] + v[b]
                best = pair if best is None else jnp.maximum(best, pair)
        group_score.append(best)
    gmax = functools.reduce(jnp.maximum, group_score)
    taken = jnp.zeros_like(gmax) > 1.0
    in_group = []
    for g in range(N_GROUPS):
        sel = jnp.logical_and(group_score[g] == gmax, jnp.logical_not(taken))
        taken = jnp.logical_or(taken, sel)
        in_group.append(sel)
    masked = [jnp.where(in_group[e // GROUP], biased[e], neg_inf) for e in range(N_EXPERTS)]

    def first_argmax(vals):
        vmax = functools.reduce(jnp.maximum, vals)
        taken_ = jnp.zeros_like(vmax) > 1.0
        picks = []
        for v in vals:
            sel = jnp.logical_and(v == vmax, jnp.logical_not(taken_))
            taken_ = jnp.logical_or(taken_, sel)
            picks.append(sel)
        return picks

    pick1 = first_argmax(masked)
    masked2 = [jnp.where(pick1[e], neg_inf, masked[e]) for e in range(N_EXPERTS)]
    pick2 = first_argmax(masked2)
    w1 = functools.reduce(jnp.add, [jnp.where(pick1[e], scores[e], 0.0) for e in range(N_EXPERTS)])
    w2 = functools.reduce(jnp.add, [jnp.where(pick2[e], scores[e], 0.0) for e in range(N_EXPERTS)])
    tot = w1 + w2
    return [jnp.where(pick1[e], w1 / tot, 0.0) + jnp.where(pick2[e], w2 / tot, 0.0)
            for e in range(N_EXPERTS)]


def _split_bf16(v):
    hi = v.astype(BF16)
    return hi, (v - hi.astype(F32)).astype(BF16)


def _merge_kernel(x_ref, ya_ref, yb_ref, wa_ref, wb_ref, wg_ref, bg_ref, wo_ref, g1_ref, b1_ref,
                  wr_hi_ref, wr_lo_ref, rb_ref, x1_ref, comb_ref, *, d):
    x = x_ref[...]
    gates = jax.nn.sigmoid(_dot(x.astype(BF16), wg_ref[...]) + bg_ref[...])
    a = _dot(ya_ref[...], wa_ref[...])
    b = _dot(yb_ref[...], wb_ref[...])
    merged = gates[:, :d] * a + gates[:, d:] * b
    mix = _dot(merged.astype(BF16), wo_ref[...])
    x1 = _layer_norm_rows(DN_ALPHA * x + mix, g1_ref[...], b1_ref[...])
    x1_ref[...] = x1
    x_hi, x_lo = _split_bf16(x1)
    nt = lambda w_, x_: lax.dot_general(w_, x_, (((1,), (1,)), ((), ())), preferred_element_type=F32)
    logits = nt(wr_hi_ref[...], x_hi) + (nt(wr_hi_ref[...], x_lo) + nt(wr_lo_ref[...], x_hi))
    sc = jax.nn.sigmoid(logits)
    bs = sc + rb_ref[...]
    scores = [sc[e:e + 1] for e in range(N_EXPERTS)]
    biased = [bs[e:e + 1] for e in range(N_EXPERTS)]
    comb_ref[...] = jnp.concatenate(_route(scores, biased), axis=0)


def _merge(x2d, ya, yb, wa, wb, wg, bg, wo, g1, b1, wr_hi, wr_lo, rb, *, tm):
    n, d = x2d.shape
    full = lambda a: pl.BlockSpec(a.shape, lambda ti: (0,) * a.ndim)
    row = lambda cols: pl.BlockSpec((tm, cols), lambda ti: (ti, 0))
    return pl.pallas_call(
        functools.partial(_merge_kernel, d=d),
        grid=(n // tm,),
        in_specs=[row(d), row(W_ATT), row(W_ATT), full(wa), full(wb), full(wg), full(bg), full(wo),
                  full(g1), full(b1), full(wr_hi), full(wr_lo), full(rb)],
        out_specs=[row(d), pl.BlockSpec((N_EXPERTS, tm), lambda ti: (0, ti))],
        out_shape=[jax.ShapeDtypeStruct((n, d), F32), jax.ShapeDtypeStruct((N_EXPERTS, n), F32)],
        compiler_params=pltpu.CompilerParams(
            dimension_semantics=("parallel",), vmem_limit_bytes=VMEM_LIMIT),
        name="merge",
    )(x2d, ya, yb, wa, wb, wg, bg, wo, g1, b1, wr_hi, wr_lo, rb)


def _moe_kernel(x_ref, comb_ref, combt_ref, l_ref, wg_ref, wu_ref, wd_ref, g2_ref, b2_ref, out_ref,
                acc_s, xb_s, rank_s, *, tm):
    e = pl.program_id(1)

    @pl.when(e == 0)
    def _():
        acc_s[...] = jnp.zeros(acc_s.shape, F32)
        xb_s[...] = x_ref[...].astype(BF16)
        picked = jnp.where(comb_ref[...] > 0.0, 1.0, 0.0).astype(BF16)
        rank_s[...] = _dot(picked, l_ref[...])

    c_row = comb_ref[pl.ds(e, 1), :]
    slot_of = jnp.where(c_row > 0.0, rank_s[pl.ds(e, 1), :], -1.0)
    n_routed = jnp.sum(jnp.where(c_row > 0.0, 1.0, 0.0))
    n_blocks = ((n_routed + (MOE_SLOTS - 1)) * (1.0 / MOE_SLOTS)).astype(jnp.int32)
    lane = lax.broadcasted_iota(jnp.int32, combt_ref.shape, 1)
    c_col = jnp.sum(jnp.where(lane == e, combt_ref[...], 0.0), axis=-1, keepdims=True)
    slot_iota = lax.broadcasted_iota(jnp.int32, (MOE_SLOTS, tm), 0).astype(F32)

    def block(b, carry):
        sel = jnp.where(slot_of == slot_iota + (b * MOE_SLOTS).astype(F32), 1.0, 0.0)
        xs = _dot(sel.astype(BF16), xb_s[...]).astype(BF16)
        h = jax.nn.silu(_dot(xs, wg_ref[0])) * _dot(xs, wu_ref[0])
        y = _dot(h.astype(BF16), wd_ref[0]).astype(BF16)
        acc_s[...] += c_col * _dot(sel.T.astype(BF16), y)
        return carry

    lax.fori_loop(0, n_blocks, block, 0)

    @pl.when(e == pl.num_programs(1) - 1)
    def _():
        out_ref[...] = _layer_norm_rows(DN_ALPHA * x_ref[...] + acc_s[...], g2_ref[...], b2_ref[...])


def _moe(x1, comb, comb_t, later, wg, wu, wd, g2, b2, *, tm):
    n, d = x1.shape
    ne, _, de = wg.shape
    full = lambda a: pl.BlockSpec(a.shape, lambda ti, e: (0,) * a.ndim)
    return pl.pallas_call(
        functools.partial(_moe_kernel, tm=tm),
        grid=(n // tm, ne),
        in_specs=[pl.BlockSpec((tm, d), lambda ti, e: (ti, 0)),
                  pl.BlockSpec((ne, tm), lambda ti, e: (0, ti)),
                  pl.BlockSpec((tm, ne), lambda ti, e: (ti, 0)),
                  full(later),
                  pl.BlockSpec((1, d, de), lambda ti, e: (e, 0, 0)),
                  pl.BlockSpec((1, d, de), lambda ti, e: (e, 0, 0)),
                  pl.BlockSpec((1, de, d), lambda ti, e: (e, 0, 0)),
                  full(g2), full(b2)],
        out_specs=pl.BlockSpec((tm, d), lambda ti, e: (ti, 0)),
        out_shape=jax.ShapeDtypeStruct((n, d), F32),
        scratch_shapes=[pltpu.VMEM((tm, d), F32), pltpu.VMEM((tm, d), BF16), pltpu.VMEM((ne, tm), F32)],
        compiler_params=pltpu.CompilerParams(
            dimension_semantics=("parallel", "arbitrary"), vmem_limit_bytes=VMEM_LIMIT),
        name="moe",
    )(x1, comb, comb_t, later, wg, wu, wd, g2, b2)


def _rope_tables(seq):
    inv_freq = ROPE_THETA ** (-jnp.arange(HALF, dtype=F32) / HALF)
    ang = inv_freq[:, None] * jnp.arange(seq, dtype=F32)[None, :]
    return jnp.cos(ang), jnp.sin(ang)


def _later_key_matrix():
    s = jnp.arange(C)[:, None]
    j = jnp.arange(C)[None, :]
    return jnp.concatenate([(j > s).astype(BF16), jnp.ones((16, C), BF16)], axis=0)


def _earlier_key_matrix():
    s = jnp.arange(C)[:, None]
    j = jnp.arange(C)[None, :]
    return jnp.concatenate([(j < s).astype(BF16), jnp.ones((16, C), BF16)], axis=0)


def kernel(x, w_in, b_gate, idx_k_norm_g, idx_k_norm_b, w_branch_a, w_branch_b, w_out, ln1_g, ln1_b,
           w_router, router_bias, exp_w_gate, exp_w_up, exp_w_down, ln2_g, ln2_b):
    bsz, seq, d = x.shape
    n = bsz * seq
    cos_t, sin_t = _rope_tables(seq)
    u = _later_key_matrix()
    t = _earlier_key_matrix()
    tok = jnp.arange(MOE_TILE)
    moe_later = (tok[:, None] < tok[None, :]).astype(BF16)
    wr_hi, wr_lo = _split_bf16(w_router.T)
    rb = router_bias.reshape(N_EXPERTS, 1)
    o_qi = 3 * W_ATT
    o_qb = o_qi + W_IDX + HEAD_DIM + IDX_HEADS
    o_g = o_qb + 3 * W_ATT
    for l in range(DEPTH):
        w = w_in[l]
        wp = _cast_bf16(jnp.concatenate(
            [w[:, :o_qi], w[:, o_qb:o_g], w[:, o_qi:o_qb], jnp.zeros((d, O_END - o_g), F32)], axis=1), rows=256)
        wg = _cast_bf16(w[:, o_g:], rows=256)
        qa, ka, va, qb, kb, vb, qi, ki, wi = _inproj(
            x, wp, cos_t, sin_t, idx_k_norm_g[l].reshape(HEAD_DIM, 1), idx_k_norm_b[l].reshape(HEAD_DIM, 1), tm=512)
        ya = _dsa(qi, ki, wi, qa, ka, va, t)
        yb = _sb(qb, kb, vb, u)
        x1, comb = _merge(
            x.reshape(n, d), ya.reshape(n, W_ATT), yb.reshape(n, W_ATT),
            w_branch_a[l].astype(BF16), w_branch_b[l].astype(BF16), wg,
            b_gate[l].reshape(1, 2 * d), w_out[l].astype(BF16), ln1_g[l].reshape(1, d), ln1_b[l].reshape(1, d),
            wr_hi, wr_lo, rb, tm=512)
        x = _moe(x1, comb, comb.T, moe_later, exp_w_gate[l].astype(BF16), exp_w_up[l].astype(BF16),
                 exp_w_down[l].astype(BF16), ln2_g[l].reshape(1, d), ln2_b[l].reshape(1, d),
                 tm=MOE_TILE).reshape(bsz, seq, d)
    return x
```

```python
import functools
import math

import jax
import jax.numpy as jnp
from jax import lax
from jax.experimental import pallas as pl
from jax.experimental.pallas import tpu as pltpu

F32 = jnp.float32
BF16 = jnp.bfloat16

LANES = 128
SUBLANES = 8
C = 256
HEAD_DIM = 64
HALF = HEAD_DIM // 2
N_HEADS = 8
N_PAIRS = N_HEADS // 2
W_ATT = N_HEADS * HEAD_DIM
IDX_HEADS = 4
W_IDX = IDX_HEADS * HEAD_DIM
N_SEL_MAX = 256
ROPE_THETA = 10000.0
LN_EPS = 1e-5
N_EXPERTS = 16
N_GROUPS = 4
GROUP = N_EXPERTS // N_GROUPS
DEPTH = 2
DN_ALPHA = (2 * DEPTH) ** 0.25
INT_MIN = -(2 ** 31)
INT_MAX = 2 ** 31 - 1
VALUE_BISECT_ROUNDS = 12
FLT_MIN_BITS = 0x00800000
CODE_INF = 0x7F800000 - (FLT_MIN_BITS - 1)
CODE_LOWEST = -(0x7F7FFFFF - (FLT_MIN_BITS - 1))
NEG_BIG = -1e30
MOE_TILE = 1024
MOE_SLOTS = 256
ATTEND_GROUPS = 1
BISECT_UNROLL = 4
V_EXT = HEAD_DIM + 16
VMEM_LIMIT = 56 * 1024 * 1024

O_QA, O_KA, O_VA, O_QB, O_KB, O_VB = (i * W_ATT for i in range(6))
O_IDX = 6 * W_ATT
W_IDX_PAD = 384
O_END = O_IDX + W_IDX_PAD


def _dot(a, b):
    return jnp.dot(a, b, preferred_element_type=F32)


def _layer_norm_rows(v, g, b):
    mu = jnp.mean(v, axis=-1, keepdims=True)
    d = v - mu
    var = jnp.mean(d * d, axis=-1, keepdims=True)
    return d * lax.rsqrt(var + LN_EPS) * g + b


def _reduce_rows(x, op):
    tiles = [x[r:r + SUBLANES] for r in range(0, x.shape[0], SUBLANES)]
    while len(tiles) > 1:
        nxt = [op(tiles[t], tiles[t + 1]) for t in range(0, len(tiles) - 1, 2)]
        if len(tiles) % 2:
            nxt.append(tiles[-1])
        tiles = nxt
    red = jnp.sum if op is jnp.add else jnp.max
    return red(tiles[0], axis=0, keepdims=True)


def _head_rows(h):
    return slice(h * HEAD_DIM, (h + 1) * HEAD_DIM)


def _pair_queries(q_t):
    zero = jnp.zeros((HEAD_DIM, C), q_t.dtype)
    blocks = []
    for p in range(N_PAIRS):
        top = jnp.concatenate([q_t[_head_rows(2 * p)], zero], axis=1)
        bot = jnp.concatenate([zero, q_t[_head_rows(2 * p + 1)]], axis=1)
        blocks.append(jnp.concatenate([top, bot], axis=0))
    return blocks


def _cast_kernel(src_ref, dst_ref):
    dst_ref[...] = src_ref[...].astype(dst_ref.dtype)


def _cast_bf16(w, *, rows):
    r, c = w.shape
    return pl.pallas_call(
        _cast_kernel,
        grid=(r // rows,),
        in_specs=[pl.BlockSpec((rows, c), lambda i: (i, 0))],
        out_specs=pl.BlockSpec((rows, c), lambda i: (i, 0)),
        out_shape=jax.ShapeDtypeStruct((r, c), BF16),
        compiler_params=pltpu.CompilerParams(dimension_semantics=("parallel",)),
        name="castw",
    )(w)


def _inproj_kernel(x_ref, w_ref, cos_ref, sin_ref, g_ref, b_ref,
                   qa_ref, ka_ref, va_ref, qb_ref, kb_ref, vb_ref, qi_ref, ki_ref, wi_ref, *, tm):
    xb = x_ref[0].astype(BF16)
    cos = cos_ref[...]
    sin = sin_ref[...]

    def proj(c0, c1):
        return _dot(xb, w_ref[:, c0:c1])

    def rope(p, scale):
        outs = []
        for h in range(p.shape[0] // HEAD_DIM):
            x1 = p[h * HEAD_DIM:h * HEAD_DIM + HALF]
            x2 = p[h * HEAD_DIM + HALF:(h + 1) * HEAD_DIM]
            outs.append((x1 * cos - x2 * sin) * scale)
            outs.append((x2 * cos + x1 * sin) * scale)
        return jnp.concatenate(outs, axis=0)

    def store_fm(ref, val):
        for c in range(tm // C):
            ref[0, c] = val[:, c * C:(c + 1) * C].astype(ref.dtype)

    qscale = math.log2(math.e) / math.sqrt(HEAD_DIM)
    store_fm(qa_ref, rope(proj(O_QA, O_KA).T, qscale))
    ka_ref[0] = rope(proj(O_KA, O_VA).T, 1.0).T.astype(ka_ref.dtype)
    store_fm(va_ref, proj(O_VA, O_QB).T)
    store_fm(qb_ref, proj(O_QB, O_KB).T * qscale)
    kb_ref[0] = proj(O_KB, O_VB).astype(kb_ref.dtype)
    store_fm(vb_ref, proj(O_VB, O_IDX).T)
    idx = proj(O_IDX, O_END).T
    store_fm(qi_ref, rope(idx[:W_IDX], 1.0))
    ki = idx[W_IDX:W_IDX + HEAD_DIM]
    mu = jnp.mean(ki, axis=0, keepdims=True)
    d = ki - mu
    var = jnp.mean(d * d, axis=0, keepdims=True)
    ki = rope(d * lax.rsqrt(var + LN_EPS) * g_ref[...] + b_ref[...], 1.0)
    ki_ref[0] = jnp.concatenate([ki, jnp.zeros_like(ki)], axis=0).T.astype(ki_ref.dtype)
    w_scale = IDX_HEADS ** -0.5 * HEAD_DIM ** -0.5
    store_fm(wi_ref, idx[W_IDX + HEAD_DIM:W_IDX + HEAD_DIM + SUBLANES] * w_scale)


def _inproj(x, w, cos_t, sin_t, g, b, *, tm):
    bsz, seq, d = x.shape
    nch = seq // C
    fm = lambda rows, dt: jax.ShapeDtypeStruct((bsz, nch, rows, C), dt)
    fm_spec = lambda rows: pl.BlockSpec((1, tm // C, rows, C), lambda bi, ti: (bi, ti, 0, 0))
    tok = lambda cols: jax.ShapeDtypeStruct((bsz, seq, cols), BF16)
    tok_spec = lambda cols: pl.BlockSpec((1, tm, cols), lambda bi, ti: (bi, ti, 0))
    full = lambda a: pl.BlockSpec(a.shape, lambda bi, ti: (0,) * a.ndim)
    return pl.pallas_call(
        functools.partial(_inproj_kernel, tm=tm),
        grid=(bsz, seq // tm),
        in_specs=[
            tok_spec(d), full(w),
            pl.BlockSpec((HALF, tm), lambda bi, ti: (0, ti)),
            pl.BlockSpec((HALF, tm), lambda bi, ti: (0, ti)),
            full(g), full(b),
        ],
        out_specs=[fm_spec(W_ATT), tok_spec(W_ATT), fm_spec(W_ATT), fm_spec(W_ATT), tok_spec(W_ATT), fm_spec(W_ATT),
                   fm_spec(W_IDX), tok_spec(LANES), fm_spec(SUBLANES)],
        out_shape=[fm(W_ATT, BF16), tok(W_ATT), fm(W_ATT, BF16), fm(W_ATT, BF16), tok(W_ATT), fm(W_ATT, BF16),
                   fm(W_IDX, BF16), tok(LANES), fm(SUBLANES, F32)],
        compiler_params=pltpu.CompilerParams(
            dimension_semantics=("parallel", "parallel"), vmem_limit_bytes=VMEM_LIMIT),
        name="inproj",
    )(x, w, cos_t, sin_t, g, b)


def _dsa_kernel(qi_ref, ki_ref, wi_ref, qa_ref, ka_ref, va_ref, t_ref, y_ref,
                key_s, bias_s, m_s, acc_s, *, n_sel):
    i = pl.program_id(1)
    nvis = i + 1
    krow = lax.broadcasted_iota(jnp.int32, (C, C), 0)
    qcol = lax.broadcasted_iota(jnp.int32, (C, C), 1)
    qlane = lax.broadcasted_iota(jnp.int32, (1, C), 1)

    qi_t = qi_ref[0, 0]
    qi_top = jnp.concatenate([qi_t[_head_rows(h)] for h in range(IDX_HEADS)], axis=1)
    qi_rhs = jnp.concatenate([qi_top, jnp.zeros_like(qi_top)], axis=0)
    wi = wi_ref[0, 0]

    def tile_tree(x, op):
        tiles = [x[r:r + SUBLANES] for r in range(0, C, SUBLANES)]
        while len(tiles) > 1:
            tiles = [op(tiles[t], tiles[t + 1]) for t in range(0, len(tiles), 2)]
        return tiles[0]

    def score_chunk(j, carry):
        lg = _dot(ki_ref[0, pl.ds(pl.multiple_of(j * C, C), C), :], qi_rhs)
        sc = jnp.maximum(lg[:, :C], 0.0) * wi[0:1]
        for h in range(1, IDX_HEADS):
            sc = sc + jnp.maximum(lg[:, h * C:(h + 1) * C], 0.0) * wi[h:h + 1]
        visible = (j - i) * C + krow <= qcol
        key_s[j] = jnp.where(visible, sc, -jnp.inf)
        return (jnp.minimum(carry[0], tile_tree(jnp.where(visible, sc, jnp.inf), jnp.minimum)),
                jnp.maximum(carry[1], tile_tree(jnp.where(visible, sc, -jnp.inf), jnp.maximum)))

    smin, smax = lax.fori_loop(0, nvis, score_chunk, (jnp.full((SUBLANES, C), jnp.inf, F32),
                                                      jnp.full((SUBLANES, C), -jnp.inf, F32)))
    smin = jnp.min(smin, axis=0, keepdims=True)
    smax = jnp.max(smax, axis=0, keepdims=True)

    def count(pred):
        def body(j, acc):
            return acc + tile_tree(pred(j * C, key_s[j]), jnp.add)
        acc = lax.fori_loop(0, nvis, body, jnp.zeros((SUBLANES, C), F32))
        return jnp.sum(acc, axis=0, keepdims=True)

    def code_to_float(code):
        mag = jnp.abs(code) + (FLT_MIN_BITS - 1)
        bits = jnp.where(code > 0, mag, jnp.where(code < 0, mag | INT_MIN, 0))
        return lax.bitcast_convert_type(bits, F32)

    def float_to_code(v):
        bits = lax.bitcast_convert_type(v, jnp.int32)
        mag = jnp.maximum((bits & INT_MAX) - (FLT_MIN_BITS - 1), 0)
        return jnp.where(bits < 0, -mag, mag)

    n_vis_q = (i * C + qlane + 1).astype(F32)
    few = n_vis_q <= n_sel
    lo0 = jnp.where(few, CODE_LOWEST, float_to_code(smin))
    hi0 = jnp.where(few, CODE_LOWEST + 1, float_to_code(smax) + 1)
    n_active0 = jnp.sum(jnp.where(hi0 - 1 > lo0, 1.0, 0.0))

    def bisect_cond(st):
        return st[4] > 0.5

    def bisect_step(st, by_value, probe=None):
        lo, hi, c_lo, c_hi = st[:4]
        mid_v = float_to_code(0.5 * code_to_float(lo) + 0.5 * code_to_float(hi))
        mid_c = (lo >> 1) + (hi >> 1) + (lo & hi & 1)
        mid = jnp.maximum(lo + 1, jnp.minimum(jnp.where(by_value, mid_v, mid_c), hi - 1))
        if probe is not None:
            mid = jnp.where(jnp.logical_and(lo < probe, probe < hi), probe, mid)
        mid_f = code_to_float(mid)
        c = count(lambda pos, k: jnp.where(k >= mid_f, 1.0, 0.0))
        ge = c >= n_sel
        exact = c == n_sel
        lo_n = jnp.where(ge, mid, lo)
        hi_n = jnp.where(exact, mid + 1, jnp.where(ge, hi, mid))
        return (lo_n, hi_n, jnp.where(ge, c, c_lo), jnp.where(ge, c_hi, c)) + tuple(st[4:])

    def bisect_key(st):
        rounds = st[5]
        for _ in range(BISECT_UNROLL):
            st = bisect_step(st, rounds < VALUE_BISECT_ROUNDS)
        lo, hi = st[0], st[1]
        return st[:4] + (jnp.sum(jnp.where(hi - 1 > lo, 1.0, 0.0)), rounds + 1)

    st0 = (lo0, hi0, n_vis_q, jnp.zeros((1, C), F32), n_active0, jnp.int32(0))
    st0 = bisect_step(bisect_step(st0, True, probe=0), True, probe=1)
    thr_code, _, c_thr, c_above, _, _ = lax.while_loop(bisect_cond, bisect_key, st0)
    thr = code_to_float(thr_code)

    need = jnp.where(c_thr > n_sel, n_sel - c_above, float(n_sel))

    def bias_chunk(j, ties_before):
        k = key_s[j]
        tie = jnp.where(k == thr, 1.0, 0.0).astype(BF16)
        sums = _dot(t_ref[...], tie)
        keep = jnp.where(ties_before + sums[:C] < need, 0.0, NEG_BIG)
        bias_s[j] = jnp.where(k > thr, 0.0, jnp.where(k == thr, keep, NEG_BIG))
        return ties_before + sums[C:C + 1]

    lax.fori_loop(0, nvis, bias_chunk, jnp.zeros((1, C), F32))

    q_blocks = _pair_queries(qa_ref[0, 0])
    m_s[...] = jnp.full(m_s.shape, NEG_BIG, F32)
    acc_s[...] = jnp.zeros(acc_s.shape, F32)
    ones_rows = jnp.ones((V_EXT - HEAD_DIM, C), BF16)

    def attend(j, carry):
        k_c = ka_ref[0, pl.ds(pl.multiple_of(j * C, C), C), :]
        bias = bias_s[j]
        for g in range(ATTEND_GROUPS):
            pairs = range(g * N_PAIRS // ATTEND_GROUPS, (g + 1) * N_PAIRS // ATTEND_GROUPS)
            cols = slice(2 * pairs[0] * C, 2 * (pairs[-1] + 1) * C)
            s = jnp.concatenate([_dot(k_c[:, p * LANES:(p + 1) * LANES], q_blocks[p]) for p in pairs], axis=1)
            s = s + jnp.concatenate([bias] * (2 * len(pairs)), axis=1)
            m_old = m_s[:, cols]
            m_new = jnp.maximum(m_old, _reduce_rows(s, jnp.maximum))
            alpha = jnp.exp2(m_old - m_new)
            pr = jnp.exp2(s - m_new).astype(BF16)
            m_s[:, cols] = m_new
            for hh in range(2 * len(pairs)):
                h = 2 * pairs[0] + hh
                v_ext = jnp.concatenate([va_ref[0, j, _head_rows(h), :], ones_rows], axis=0)
                acc_s[h] = acc_s[h] * alpha[:, hh * C:(hh + 1) * C] + _dot(v_ext, pr[:, hh * C:(hh + 1) * C])
        return carry

    lax.fori_loop(0, nvis, attend, 0)
    outs = []
    for h in range(N_HEADS):
        a = acc_s[h]
        outs.append(a[:HEAD_DIM] / a[HEAD_DIM:HEAD_DIM + 1])
    y_ref[0] = jnp.concatenate(outs, axis=0).T.astype(y_ref.dtype)


def _dsa(qi, ki, wi, qa, ka, va, t):
    bsz, nch, _, _ = qa.shape
    seq = nch * C
    n_sel = min(N_SEL_MAX, seq // 4)
    qspec = lambda rows: pl.BlockSpec((1, 1, rows, C), lambda bi, qi_: (bi, qi_, 0, 0))
    return pl.pallas_call(
        functools.partial(_dsa_kernel, n_sel=n_sel),
        grid=(bsz, nch),
        in_specs=[qspec(W_IDX),
                  pl.BlockSpec((1, seq, LANES), lambda bi, qi_: (bi, 0, 0)),
                  qspec(SUBLANES), qspec(W_ATT),
                  pl.BlockSpec((1, seq, W_ATT), lambda bi, qi_: (bi, 0, 0)),
                  pl.BlockSpec((1, nch, W_ATT, C), lambda bi, qi_: (bi, 0, 0, 0)),
                  pl.BlockSpec(t.shape, lambda bi, qi_: (0, 0))],
        out_specs=pl.BlockSpec((1, C, W_ATT), lambda bi, qi_: (bi, qi_, 0)),
        out_shape=jax.ShapeDtypeStruct((bsz, seq, W_ATT), BF16),
        scratch_shapes=[
            pltpu.VMEM((nch, C, C), F32),
            pltpu.VMEM((nch, C, C), F32),
            pltpu.VMEM((1, N_HEADS * C), F32),
            pltpu.VMEM((N_HEADS, V_EXT, C), F32),
        ],
        compiler_params=pltpu.CompilerParams(
            dimension_semantics=("parallel", "arbitrary"), vmem_limit_bytes=VMEM_LIMIT),
        name="dsa",
    )(qi, ki, wi, qa, ka, va, t)


def _sb_kernel(q_ref, k_ref, v_ref, u_ref, y_ref, acc_s, carry_s):
    i = pl.program_id(1)
    q_blocks = _pair_queries(q_ref[0, 0])
    acc_s[...] = jnp.zeros(acc_s.shape, F32)
    carry_s[...] = jnp.zeros(carry_s.shape, F32)

    def step(j, diagonal):
        k_c = k_ref[0, pl.ds(pl.multiple_of(j * C, C), C), :]
        z = jnp.concatenate([_dot(k_c[:, p * LANES:(p + 1) * LANES], q_blocks[p]) for p in range(N_PAIRS)], axis=1)
        sp = jnp.log2(1.0 + jnp.exp2(-jnp.abs(z)))
        log_beta = jnp.minimum(z, 0.0) - sp
        log_rest = log_beta - z
        if diagonal:
            before = (lax.broadcasted_iota(jnp.int32, z.shape, 0)
                      < (lax.broadcasted_iota(jnp.int32, z.shape, 1) & (C - 1)))
            log_rest = jnp.where(before, log_rest, 0.0)
        sums = _dot(u_ref[...], log_rest.astype(BF16))
        carry = carry_s[...]
        att = jnp.exp2(log_beta + (sums[:C] + carry))
        if diagonal:
            att = jnp.where(before, att, 0.0)
        carry_s[...] = carry + sums[C:C + 1]
        att = att.astype(BF16)
        for h in range(N_HEADS):
            acc_s[h] += _dot(v_ref[0, j, _head_rows(h), :], att[:, h * C:(h + 1) * C])

    step(i, True)

    def earlier(it, carry):
        step(i - 1 - it, False)
        return carry

    lax.fori_loop(0, i, earlier, 0)
    y_ref[0] = jnp.concatenate([acc_s[h] for h in range(N_HEADS)], axis=0).T.astype(y_ref.dtype)


def _sb(q, k, v, u):
    bsz, nch, _, _ = q.shape
    seq = nch * C
    return pl.pallas_call(
        _sb_kernel,
        grid=(bsz, nch),
        in_specs=[pl.BlockSpec((1, 1, W_ATT, C), lambda bi, qi_: (bi, qi_, 0, 0)),
                  pl.BlockSpec((1, seq, W_ATT), lambda bi, qi_: (bi, 0, 0)),
                  pl.BlockSpec((1, nch, W_ATT, C), lambda bi, qi_: (bi, 0, 0, 0)),
                  pl.BlockSpec(u.shape, lambda bi, qi_: (0, 0))],
        out_specs=pl.BlockSpec((1, C, W_ATT), lambda bi, qi_: (bi, qi_, 0)),
        out_shape=jax.ShapeDtypeStruct((bsz, seq, W_ATT), BF16),
        scratch_shapes=[pltpu.VMEM((N_HEADS, HEAD_DIM, C), F32),
                        pltpu.VMEM((1, N_HEADS * C), F32)],
        compiler_params=pltpu.CompilerParams(
            dimension_semantics=("parallel", "arbitrary"), vmem_limit_bytes=VMEM_LIMIT),
        name="stickbreak",
    )(q, k, v, u)


def _route(scores, biased):
    neg_inf = -jnp.inf
    group_score = []
    for g in range(N_GROUPS):
        v = biased[g * GROUP:(g + 1) * GROUP]
        best = None
        for a in range(GROUP):
            for b in range(a + 1, GROUP):
                pair = v[a] + v[b]
                best = pair if best is None else jnp.maximum(best, pair)
        group_score.append(best)
    gmax = functools.reduce(jnp.maximum, group_score)
    taken = jnp.zeros_like(gmax) > 1.0
    in_group = []
    for g in range(N_GROUPS):
        sel = jnp.logical_and(group_score[g] == gmax, jnp.logical_not(taken))
        taken = jnp.logical_or(taken, sel)
        in_group.append(sel)
    masked = [jnp.where(in_group[e // GROUP], biased[e], neg_inf) for e in range(N_EXPERTS)]

    def first_argmax(vals):
        vmax = functools.reduce(jnp.maximum, vals)
        taken_ = jnp.zeros_like(vmax) > 1.0
        picks = []
        for v in vals:
            sel = jnp.logical_and(v == vmax, jnp.logical_not(taken_))
            taken_ = jnp.logical_or(taken_, sel)
            picks.append(sel)
        return picks

    pick1 = first_argmax(masked)
    masked2 = [jnp.where(pick1[e], neg_inf, masked[e]) for e in range(N_EXPERTS)]
    pick2 = first_argmax(masked2)
    w1 = functools.reduce(jnp.add, [jnp.where(pick1[e], scores[e], 0.0) for e in range(N_EXPERTS)])
    w2 = functools.reduce(jnp.add, [jnp.where(pick2[e], scores[e], 0.0) for e in range(N_EXPERTS)])
    tot = w1 + w2
    return [jnp.where(pick1[e], w1 / tot, 0.0) + jnp.where(pick2[e], w2 / tot, 0.0)
            for e in range(N_EXPERTS)]


def _split_bf16(v):
    hi = v.astype(BF16)
    return hi, (v - hi.astype(F32)).astype(BF16)


def _merge_kernel(x_ref, ya_ref, yb_ref, wa_ref, wb_ref, wg_ref, bg_ref, wo_ref, g1_ref, b1_ref,
                  wr_hi_ref, wr_lo_ref, rb_ref, x1_ref, comb_ref, *, d):
    x = x_ref[...]
    gates = jax.nn.sigmoid(_dot(x.astype(BF16), wg_ref[...]) + bg_ref[...])
    a = _dot(ya_ref[...], wa_ref[...])
    b = _dot(yb_ref[...], wb_ref[...])
    merged = gates[:, :d] * a + gates[:, d:] * b
    mix = _dot(merged.astype(BF16), wo_ref[...])
    x1 = _layer_norm_rows(DN_ALPHA * x + mix, g1_ref[...], b1_ref[...])
    x1_ref[...] = x1
    x_hi, x_lo = _split_bf16(x1)
    nt = lambda w_, x_: lax.dot_general(w_, x_, (((1,), (1,)), ((), ())), preferred_element_type=F32)
    logits = nt(wr_hi_ref[...], x_hi) + (nt(wr_hi_ref[...], x_lo) + nt(wr_lo_ref[...], x_hi))
    sc = jax.nn.sigmoid(logits)
    bs = sc + rb_ref[...]
    scores = [sc[e:e + 1] for e in range(N_EXPERTS)]
    biased = [bs[e:e + 1] for e in range(N_EXPERTS)]
    comb_ref[...] = jnp.concatenate(_route(scores, biased), axis=0)


def _merge(x2d, ya, yb, wa, wb, wg, bg, wo, g1, b1, wr_hi, wr_lo, rb, *, tm):
    n, d = x2d.shape
    full = lambda a: pl.BlockSpec(a.shape, lambda ti: (0,) * a.ndim)
    row = lambda cols: pl.BlockSpec((tm, cols), lambda ti: (ti, 0))
    return pl.pallas_call(
        functools.partial(_merge_kernel, d=d),
        grid=(n // tm,),
        in_specs=[row(d), row(W_ATT), row(W_ATT), full(wa), full(wb), full(wg), full(bg), full(wo),
                  full(g1), full(b1), full(wr_hi), full(wr_lo), full(rb)],
        out_specs=[row(d), pl.BlockSpec((N_EXPERTS, tm), lambda ti: (0, ti))],
        out_shape=[jax.ShapeDtypeStruct((n, d), F32), jax.ShapeDtypeStruct((N_EXPERTS, n), F32)],
        compiler_params=pltpu.CompilerParams(
            dimension_semantics=("parallel",), vmem_limit_bytes=VMEM_LIMIT),
        name="merge",
    )(x2d, ya, yb, wa, wb, wg, bg, wo, g1, b1, wr_hi, wr_lo, rb)


def _moe_kernel(x_ref, comb_ref, combt_ref, l_ref, wg_ref, wu_ref, wd_ref, g2_ref, b2_ref, out_ref,
                acc_s, xb_s, rank_s, *, tm):
    e = pl.program_id(1)

    @pl.when(e == 0)
    def _():
        acc_s[...] = jnp.zeros(acc_s.shape, F32)
        xb_s[...] = x_ref[...].astype(BF16)
        picked = jnp.where(comb_ref[...] > 0.0, 1.0, 0.0).astype(BF16)
        rank_s[...] = _dot(picked, l_ref[...])

    c_row = comb_ref[pl.ds(e, 1), :]
    slot_of = jnp.where(c_row > 0.0, rank_s[pl.ds(e, 1), :], -1.0)
    n_routed = jnp.sum(jnp.where(c_row > 0.0, 1.0, 0.0))
    n_blocks = ((n_routed + (MOE_SLOTS - 1)) * (1.0 / MOE_SLOTS)).astype(jnp.int32)
    lane = lax.broadcasted_iota(jnp.int32, combt_ref.shape, 1)
    c_col = jnp.sum(jnp.where(lane == e, combt_ref[...], 0.0), axis=-1, keepdims=True)
    slot_iota = lax.broadcasted_iota(jnp.int32, (MOE_SLOTS, tm), 0).astype(F32)

    def block(b, carry):
        sel = jnp.where(slot_of == slot_iota + (b * MOE_SLOTS).astype(F32), 1.0, 0.0)
        xs = _dot(sel.astype(BF16), xb_s[...]).astype(BF16)
        h = jax.nn.silu(_dot(xs, wg_ref[0])) * _dot(xs, wu_ref[0])
        y = _dot(h.astype(BF16), wd_ref[0]).astype(BF16)
        acc_s[...] += c_col * _dot(sel.T.astype(BF16), y)
        return carry

    lax.fori_loop(0, n_blocks, block, 0)

    @pl.when(e == pl.num_programs(1) - 1)
    def _():
        out_ref[...] = _layer_norm_rows(DN_ALPHA * x_ref[...] + acc_s[...], g2_ref[...], b2_ref[...])


def _moe(x1, comb, comb_t, later, wg, wu, wd, g2, b2, *, tm):
    n, d = x1.shape
    ne, _, de = wg.shape
    full = lambda a: pl.BlockSpec(a.shape, lambda ti, e: (0,) * a.ndim)
    return pl.pallas_call(
        functools.partial(_moe_kernel, tm=tm),
        grid=(n // tm, ne),
        in_specs=[pl.BlockSpec((tm, d), lambda ti, e: (ti, 0)),
                  pl.BlockSpec((ne, tm), lambda ti, e: (0, ti)),
                  pl.BlockSpec((tm, ne), lambda ti, e: (ti, 0)),
                  full(later),
                  pl.BlockSpec((1, d, de), lambda ti, e: (e, 0, 0)),
                  pl.BlockSpec((1, d, de), lambda ti, e: (e, 0, 0)),
                  pl.BlockSpec((1, de, d), lambda ti, e: (e, 0, 0)),
                  full(g2), full(b2)],
        out_specs=pl.BlockSpec((tm, d), lambda ti, e: (ti, 0)),
        out_shape=jax.ShapeDtypeStruct((n, d), F32),
        scratch_shapes=[pltpu.VMEM((tm, d), F32), pltpu.VMEM((tm, d), BF16), pltpu.VMEM((ne, tm), F32)],
        compiler_params=pltpu.CompilerParams(
            dimension_semantics=("parallel", "arbitrary"), vmem_limit_bytes=VMEM_LIMIT),
        name="moe",
    )(x1, comb, comb_t, later, wg, wu, wd, g2, b2)


def _rope_tables(seq):
    inv_freq = ROPE_THETA ** (-jnp.arange(HALF, dtype=F32) / HALF)
    ang = inv_freq[:, None] * jnp.arange(seq, dtype=F32)[None, :]
    return jnp.cos(ang), jnp.sin(ang)


def _later_key_matrix():
    s = jnp.arange(C)[:, None]
    j = jnp.arange(C)[None, :]
    return jnp.concatenate([(j > s).astype(BF16), jnp.ones((16, C), BF16)], axis=0)


def _earlier_key_matrix():
    s = jnp.arange(C)[:, None]
    j = jnp.arange(C)[None, :]
    return jnp.concatenate([(j < s).astype(BF16), jnp.ones((16, C), BF16)], axis=0)


def kernel(x, w_in, b_gate, idx_k_norm_g, idx_k_norm_b, w_branch_a, w_branch_b, w_out, ln1_g, ln1_b,
           w_router, router_bias, exp_w_gate, exp_w_up, exp_w_down, ln2_g, ln2_b):
    bsz, seq, d = x.shape
    n = bsz * seq
    cos_t, sin_t = _rope_tables(seq)
    u = _later_key_matrix()
    t = _earlier_key_matrix()
    tok = jnp.arange(MOE_TILE)
    moe_later = (tok[:, None] < tok[None, :]).astype(BF16)
    wr_hi, wr_lo = _split_bf16(w_router.T)
    rb = router_bias.reshape(N_EXPERTS, 1)
    o_qi = 3 * W_ATT
    o_qb = o_qi + W_IDX + HEAD_DIM + IDX_HEADS
    o_g = o_qb + 3 * W_ATT
    for l in range(DEPTH):
        w = w_in[l]
        wp = _cast_bf16(jnp.concatenate(
            [w[:, :o_qi], w[:, o_qb:o_g], w[:, o_qi:o_qb], jnp.zeros((d, O_END - o_g), F32)], axis=1), rows=256)
        wg = _cast_bf16(w[:, o_g:], rows=256)
        qa, ka, va, qb, kb, vb, qi, ki, wi = _inproj(
            x, wp, cos_t, sin_t, idx_k_norm_g[l].reshape(HEAD_DIM, 1), idx_k_norm_b[l].reshape(HEAD_DIM, 1), tm=512)
        ya = _dsa(qi, ki, wi, qa, ka, va, t)
        yb = _sb(qb, kb, vb, u)
        x1, comb = _merge(
            x.reshape(n, d), ya.reshape(n, W_ATT), yb.reshape(n, W_ATT),
            w_branch_a[l].astype(BF16), w_branch_b[l].astype(BF16), wg,
            b_gate[l].reshape(1, 2 * d), w_out[l].astype(BF16), ln1_g[l].reshape(1, d), ln1_b[l].reshape(1, d),
            wr_hi, wr_lo, rb, tm=512)
        x = _moe(x1, comb, comb.T, moe_later, exp_w_gate[l].astype(BF16), exp_w_up[l].astype(BF16),
                 exp_w_down[l].astype(BF16), ln2_g[l].reshape(1, d), ln2_b[l].reshape(1, d),
                 tm=MOE_TILE).reshape(bsz, seq, d)
    return x
```

```python
import functools
import math

import jax
import jax.numpy as jnp
from jax import lax
from jax.experimental import pallas as pl
from jax.experimental.pallas import tpu as pltpu

F32 = jnp.float32
BF16 = jnp.bfloat16

LANES = 128
SUBLANES = 8
C = 256
HEAD_DIM = 64
HALF = HEAD_DIM // 2
N_HEADS = 8
N_PAIRS = N_HEADS // 2
W_ATT = N_HEADS * HEAD_DIM
IDX_HEADS = 4
W_IDX = IDX_HEADS * HEAD_DIM
N_SEL_MAX = 256
ROPE_THETA = 10000.0
LN_EPS = 1e-5
N_EXPERTS = 16
N_GROUPS = 4
GROUP = N_EXPERTS // N_GROUPS
DEPTH = 2
DN_ALPHA = (2 * DEPTH) ** 0.25
INT_MIN = -(2 ** 31)
INT_MAX = 2 ** 31 - 1
VALUE_BISECT_ROUNDS = 12
FLT_MIN_BITS = 0x00800000
CODE_INF = 0x7F800000 - (FLT_MIN_BITS - 1)
CODE_LOWEST = -(0x7F7FFFFF - (FLT_MIN_BITS - 1))
NEG_BIG = -1e30
MOE_TILE = 1024
MOE_SLOTS = 160
MOE_SCATTER_K = 256
BISECT_UNROLL = 4
V_EXT = HEAD_DIM + 16
VMEM_LIMIT = 56 * 1024 * 1024

O_QA, O_KA, O_VA, O_QB, O_KB, O_VB = (i * W_ATT for i in range(6))
O_IDX = 6 * W_ATT
W_IDX_PAD = 384
O_END = O_IDX + W_IDX_PAD


def _dot(a, b):
    return jnp.dot(a, b, preferred_element_type=F32)


def _layer_norm_rows(v, g, b):
    mu = jnp.mean(v, axis=-1, keepdims=True)
    d = v - mu
    var = jnp.mean(d * d, axis=-1, keepdims=True)
    return d * lax.rsqrt(var + LN_EPS) * g + b


def _reduce_rows(x, op):
    tiles = [x[r:r + SUBLANES] for r in range(0, x.shape[0], SUBLANES)]
    while len(tiles) > 1:
        nxt = [op(tiles[t], tiles[t + 1]) for t in range(0, len(tiles) - 1, 2)]
        if len(tiles) % 2:
            nxt.append(tiles[-1])
        tiles = nxt
    red = jnp.sum if op is jnp.add else jnp.max
    return red(tiles[0], axis=0, keepdims=True)


def _head_rows(h):
    return slice(h * HEAD_DIM, (h + 1) * HEAD_DIM)


def _pair_queries(q_t):
    zero = jnp.zeros((HEAD_DIM, C), q_t.dtype)
    blocks = []
    for p in range(N_PAIRS):
        top = jnp.concatenate([q_t[_head_rows(2 * p)], zero], axis=1)
        bot = jnp.concatenate([zero, q_t[_head_rows(2 * p + 1)]], axis=1)
        blocks.append(jnp.concatenate([top, bot], axis=0))
    return blocks


def _cast_kernel(src_ref, dst_ref):
    dst_ref[...] = src_ref[...].astype(dst_ref.dtype)


def _cast_bf16(w, *, rows):
    r, c = w.shape
    return pl.pallas_call(
        _cast_kernel,
        grid=(r // rows,),
        in_specs=[pl.BlockSpec((rows, c), lambda i: (i, 0))],
        out_specs=pl.BlockSpec((rows, c), lambda i: (i, 0)),
        out_shape=jax.ShapeDtypeStruct((r, c), BF16),
        compiler_params=pltpu.CompilerParams(dimension_semantics=("parallel",)),
        name="castw",
    )(w)


def _inproj_kernel(x_ref, w_ref, cos_ref, sin_ref, g_ref, b_ref,
                   qa_ref, ka_ref, va_ref, qb_ref, kb_ref, vb_ref, qi_ref, ki_ref, wi_ref, *, tm):
    xb = x_ref[0].astype(BF16)
    cos = cos_ref[...]
    sin = sin_ref[...]

    def proj(c0, c1):
        return _dot(xb, w_ref[:, c0:c1])

    def rope(p, scale):
        outs = []
        for h in range(p.shape[0] // HEAD_DIM):
            x1 = p[h * HEAD_DIM:h * HEAD_DIM + HALF]
            x2 = p[h * HEAD_DIM + HALF:(h + 1) * HEAD_DIM]
            outs.append((x1 * cos - x2 * sin) * scale)
            outs.append((x2 * cos + x1 * sin) * scale)
        return jnp.concatenate(outs, axis=0)

    def store_fm(ref, val):
        for c in range(tm // C):
            ref[0, c] = val[:, c * C:(c + 1) * C].astype(ref.dtype)

    qscale = math.log2(math.e) / math.sqrt(HEAD_DIM)
    store_fm(qa_ref, rope(proj(O_QA, O_KA).T, qscale))
    ka_ref[0] = rope(proj(O_KA, O_VA).T, 1.0).T.astype(ka_ref.dtype)
    store_fm(va_ref, proj(O_VA, O_QB).T)
    store_fm(qb_ref, proj(O_QB, O_KB).T * qscale)
    kb_ref[0] = proj(O_KB, O_VB).astype(kb_ref.dtype)
    store_fm(vb_ref, proj(O_VB, O_IDX).T)
    idx = proj(O_IDX, O_END).T
    store_fm(qi_ref, rope(idx[:W_IDX], 1.0))
    ki = idx[W_IDX:W_IDX + HEAD_DIM]
    mu = jnp.mean(ki, axis=0, keepdims=True)
    d = ki - mu
    var = jnp.mean(d * d, axis=0, keepdims=True)
    ki = rope(d * lax.rsqrt(var + LN_EPS) * g_ref[...] + b_ref[...], 1.0)
    ki_ref[0] = jnp.concatenate([ki, jnp.zeros_like(ki)], axis=0).T.astype(ki_ref.dtype)
    w_scale = IDX_HEADS ** -0.5 * HEAD_DIM ** -0.5
    store_fm(wi_ref, idx[W_IDX + HEAD_DIM:W_IDX + HEAD_DIM + SUBLANES] * w_scale)


def _inproj(x, w, cos_t, sin_t, g, b, *, tm):
    bsz, seq, d = x.shape
    nch = seq // C
    fm = lambda rows, dt: jax.ShapeDtypeStruct((bsz, nch, rows, C), dt)
    fm_spec = lambda rows: pl.BlockSpec((1, tm // C, rows, C), lambda bi, ti: (bi, ti, 0, 0))
    tok = lambda cols: jax.ShapeDtypeStruct((bsz, seq, cols), BF16)
    tok_spec = lambda cols: pl.BlockSpec((1, tm, cols), lambda bi, ti: (bi, ti, 0))
    full = lambda a: pl.BlockSpec(a.shape, lambda bi, ti: (0,) * a.ndim)
    return pl.pallas_call(
        functools.partial(_inproj_kernel, tm=tm),
        grid=(bsz, seq // tm),
        in_specs=[
            tok_spec(d), full(w),
            pl.BlockSpec((HALF, tm), lambda bi, ti: (0, ti)),
            pl.BlockSpec((HALF, tm), lambda bi, ti: (0, ti)),
            full(g), full(b),
        ],
        out_specs=[fm_spec(W_ATT), tok_spec(W_ATT), fm_spec(W_ATT), fm_spec(W_ATT), tok_spec(W_ATT), fm_spec(W_ATT),
                   fm_spec(W_IDX), tok_spec(LANES), fm_spec(SUBLANES)],
        out_shape=[fm(W_ATT, BF16), tok(W_ATT), fm(W_ATT, BF16), fm(W_ATT, BF16), tok(W_ATT), fm(W_ATT, BF16),
                   fm(W_IDX, BF16), tok(LANES), fm(SUBLANES, F32)],
        compiler_params=pltpu.CompilerParams(
            dimension_semantics=("parallel", "parallel"), vmem_limit_bytes=VMEM_LIMIT),
        name="inproj",
    )(x, w, cos_t, sin_t, g, b)


def _dsa_kernel(qi_ref, ki_ref, wi_ref, qa_ref, ka_ref, va_ref, t_ref, y_ref,
                key_s, bias_s, m_s, acc_s, *, n_sel):
    i = pl.program_id(1)
    nvis = i + 1
    krow = lax.broadcasted_iota(jnp.int32, (C, C), 0)
    qcol = lax.broadcasted_iota(jnp.int32, (C, C), 1)
    qlane = lax.broadcasted_iota(jnp.int32, (1, C), 1)

    qi_t = qi_ref[0, 0]
    qi_top = jnp.concatenate([qi_t[_head_rows(h)] for h in range(IDX_HEADS)], axis=1)
    qi_rhs = jnp.concatenate([qi_top, jnp.zeros_like(qi_top)], axis=0)
    wi = wi_ref[0, 0]

    def tile_tree(x, op):
        tiles = [x[r:r + SUBLANES] for r in range(0, C, SUBLANES)]
        while len(tiles) > 1:
            tiles = [op(tiles[t], tiles[t + 1]) for t in range(0, len(tiles), 2)]
        return tiles[0]

    def score_chunk(j, carry):
        lg = _dot(ki_ref[0, pl.ds(pl.multiple_of(j * C, C), C), :], qi_rhs)
        sc = jnp.maximum(lg[:, :C], 0.0) * wi[0:1]
        for h in range(1, IDX_HEADS):
            sc = sc + jnp.maximum(lg[:, h * C:(h + 1) * C], 0.0) * wi[h:h + 1]
        visible = (j - i) * C + krow <= qcol
        key_s[j] = jnp.where(visible, sc, -jnp.inf)
        return (jnp.minimum(carry[0], tile_tree(jnp.where(visible, sc, jnp.inf), jnp.minimum)),
                jnp.maximum(carry[1], tile_tree(jnp.where(visible, sc, -jnp.inf), jnp.maximum)))

    smin, smax = lax.fori_loop(0, nvis, score_chunk, (jnp.full((SUBLANES, C), jnp.inf, F32),
                                                      jnp.full((SUBLANES, C), -jnp.inf, F32)))
    smin = jnp.min(smin, axis=0, keepdims=True)
    smax = jnp.max(smax, axis=0, keepdims=True)

    def count(pred):
        def body(j, acc):
            return acc + tile_tree(pred(j * C, key_s[j]), jnp.add)
        acc = lax.fori_loop(0, nvis, body, jnp.zeros((SUBLANES, C), F32))
        return jnp.sum(acc, axis=0, keepdims=True)

    def code_to_float(code):
        mag = jnp.abs(code) + (FLT_MIN_BITS - 1)
        bits = jnp.where(code > 0, mag, jnp.where(code < 0, mag | INT_MIN, 0))
        return lax.bitcast_convert_type(bits, F32)

    def float_to_code(v):
        bits = lax.bitcast_convert_type(v, jnp.int32)
        mag = jnp.maximum((bits & INT_MAX) - (FLT_MIN_BITS - 1), 0)
        return jnp.where(bits < 0, -mag, mag)

    n_vis_q = (i * C + qlane + 1).astype(F32)
    few = n_vis_q <= n_sel
    lo0 = jnp.where(few, CODE_LOWEST, float_to_code(smin))
    hi0 = jnp.where(few, CODE_LOWEST + 1, float_to_code(smax) + 1)
    n_active0 = jnp.sum(jnp.where(hi0 - 1 > lo0, 1.0, 0.0))

    def bisect_cond(st):
        return st[4] > 0.5

    def bisect_step(st, by_value, probe=None):
        lo, hi, c_lo, c_hi = st[:4]
        mid_v = float_to_code(0.5 * code_to_float(lo) + 0.5 * code_to_float(hi))
        mid_c = (lo >> 1) + (hi >> 1) + (lo & hi & 1)
        mid = jnp.maximum(lo + 1, jnp.minimum(jnp.where(by_value, mid_v, mid_c), hi - 1))
        if probe is not None:
            mid = jnp.where(jnp.logical_and(lo < probe, probe < hi), probe, mid)
        mid_f = code_to_float(mid)
        c = count(lambda pos, k: jnp.where(k >= mid_f, 1.0, 0.0))
        ge = c >= n_sel
        exact = c == n_sel
        lo_n = jnp.where(ge, mid, lo)
        hi_n = jnp.where(exact, mid + 1, jnp.where(ge, hi, mid))
        return (lo_n, hi_n, jnp.where(ge, c, c_lo), jnp.where(ge, c_hi, c)) + tuple(st[4:])

    def bisect_key(st):
        rounds = st[5]
        for _ in range(BISECT_UNROLL):
            st = bisect_step(st, rounds < VALUE_BISECT_ROUNDS)
        lo, hi = st[0], st[1]
        return st[:4] + (jnp.sum(jnp.where(hi - 1 > lo, 1.0, 0.0)), rounds + 1)

    st0 = (lo0, hi0, n_vis_q, jnp.zeros((1, C), F32), n_active0, jnp.int32(0))
    st0 = bisect_step(bisect_step(st0, True, probe=0), True, probe=1)
    thr_code, _, c_thr, c_above, _, _ = lax.while_loop(bisect_cond, bisect_key, st0)
    thr = code_to_float(thr_code)

    need = jnp.where(c_thr > n_sel, n_sel - c_above, float(n_sel))

    def bias_chunk(j, ties_before):
        k = key_s[j]
        tie = jnp.where(k == thr, 1.0, 0.0).astype(BF16)
        sums = _dot(t_ref[...], tie)
        keep = jnp.where(ties_before + sums[:C] < need, 0.0, NEG_BIG)
        bias_s[j] = jnp.where(k > thr, 0.0, jnp.where(k == thr, keep, NEG_BIG))
        return ties_before + sums[C:C + 1]

    lax.fori_loop(0, nvis, bias_chunk, jnp.zeros((1, C), F32))

    q_blocks = _pair_queries(qa_ref[0, 0])
    m_s[...] = jnp.full(m_s.shape, NEG_BIG, F32)
    acc_s[...] = jnp.zeros(acc_s.shape, F32)
    ones_rows = jnp.ones((V_EXT - HEAD_DIM, C), BF16)

    def attend(j, carry):
        k_c = ka_ref[0, pl.ds(pl.multiple_of(j * C, C), C), :]
        s = jnp.concatenate([_dot(k_c[:, p * LANES:(p + 1) * LANES], q_blocks[p]) for p in range(N_PAIRS)], axis=1)
        s = s + jnp.concatenate([bias_s[j]] * N_HEADS, axis=1)
        m_old = m_s[...]
        m_new = jnp.maximum(m_old, _reduce_rows(s, jnp.maximum))
        alpha = jnp.exp2(m_old - m_new)
        pr = jnp.exp2(s - m_new).astype(BF16)
        m_s[...] = m_new
        for h in range(N_HEADS):
            v_ext = jnp.concatenate([va_ref[0, j, _head_rows(h), :], ones_rows], axis=0)
            acc_s[h] = acc_s[h] * alpha[:, h * C:(h + 1) * C] + _dot(v_ext, pr[:, h * C:(h + 1) * C])
        return carry

    lax.fori_loop(0, nvis, attend, 0)
    outs = []
    for h in range(N_HEADS):
        a = acc_s[h]
        outs.append(a[:HEAD_DIM] / a[HEAD_DIM:HEAD_DIM + 1])
    y_ref[0] = jnp.concatenate(outs, axis=0).T.astype(y_ref.dtype)


def _dsa(qi, ki, wi, qa, ka, va, t):
    bsz, nch, _, _ = qa.shape
    seq = nch * C
    n_sel = min(N_SEL_MAX, seq // 4)
    qspec = lambda rows: pl.BlockSpec((1, 1, rows, C), lambda bi, qi_: (bi, qi_, 0, 0))
    return pl.pallas_call(
        functools.partial(_dsa_kernel, n_sel=n_sel),
        grid=(bsz, nch),
        in_specs=[qspec(W_IDX),
                  pl.BlockSpec((1, seq, LANES), lambda bi, qi_: (bi, 0, 0)),
                  qspec(SUBLANES), qspec(W_ATT),
                  pl.BlockSpec((1, seq, W_ATT), lambda bi, qi_: (bi, 0, 0)),
                  pl.BlockSpec((1, nch, W_ATT, C), lambda bi, qi_: (bi, 0, 0, 0)),
                  pl.BlockSpec(t.shape, lambda bi, qi_: (0, 0))],
        out_specs=pl.BlockSpec((1, C, W_ATT), lambda bi, qi_: (bi, qi_, 0)),
        out_shape=jax.ShapeDtypeStruct((bsz, seq, W_ATT), BF16),
        scratch_shapes=[
            pltpu.VMEM((nch, C, C), F32),
            pltpu.VMEM((nch, C, C), F32),
            pltpu.VMEM((1, N_HEADS * C), F32),
            pltpu.VMEM((N_HEADS, V_EXT, C), F32),
        ],
        compiler_params=pltpu.CompilerParams(
            dimension_semantics=("parallel", "arbitrary"), vmem_limit_bytes=VMEM_LIMIT),
        name="dsa",
    )(qi, ki, wi, qa, ka, va, t)


def _sb_kernel(q_ref, k_ref, v_ref, u_ref, y_ref, acc_s, carry_s):
    i = pl.program_id(1)
    q_blocks = _pair_queries(q_ref[0, 0])
    acc_s[...] = jnp.zeros(acc_s.shape, F32)
    carry_s[...] = jnp.zeros(carry_s.shape, F32)

    def step(j, diagonal):
        k_c = k_ref[0, pl.ds(pl.multiple_of(j * C, C), C), :]
        z = jnp.concatenate([_dot(k_c[:, p * LANES:(p + 1) * LANES], q_blocks[p]) for p in range(N_PAIRS)], axis=1)
        sp = jnp.log2(1.0 + jnp.exp2(-jnp.abs(z)))
        log_beta = jnp.minimum(z, 0.0) - sp
        log_rest = log_beta - z
        if diagonal:
            before = (lax.broadcasted_iota(jnp.int32, z.shape, 0)
                      < (lax.broadcasted_iota(jnp.int32, z.shape, 1) & (C - 1)))
            log_rest = jnp.where(before, log_rest, 0.0)
        sums = _dot(u_ref[...], log_rest.astype(BF16))
        carry = carry_s[...]
        att = jnp.exp2(log_beta + (sums[:C] + carry))
        if diagonal:
            att = jnp.where(before, att, 0.0)
        carry_s[...] = carry + sums[C:C + 1]
        att = att.astype(BF16)
        for h in range(N_HEADS):
            acc_s[h] += _dot(v_ref[0, j, _head_rows(h), :], att[:, h * C:(h + 1) * C])

    step(i, True)

    def earlier(it, carry):
        step(i - 1 - it, False)
        return carry

    lax.fori_loop(0, i, earlier, 0)
    y_ref[0] = jnp.concatenate([acc_s[h] for h in range(N_HEADS)], axis=0).T.astype(y_ref.dtype)


def _sb(q, k, v, u):
    bsz, nch, _, _ = q.shape
    seq = nch * C
    return pl.pallas_call(
        _sb_kernel,
        grid=(bsz, nch),
        in_specs=[pl.BlockSpec((1, 1, W_ATT, C), lambda bi, qi_: (bi, qi_, 0, 0)),
                  pl.BlockSpec((1, seq, W_ATT), lambda bi, qi_: (bi, 0, 0)),
                  pl.BlockSpec((1, nch, W_ATT, C), lambda bi, qi_: (bi, 0, 0, 0)),
                  pl.BlockSpec(u.shape, lambda bi, qi_: (0, 0))],
        out_specs=pl.BlockSpec((1, C, W_ATT), lambda bi, qi_: (bi, qi_, 0)),
        out_shape=jax.ShapeDtypeStruct((bsz, seq, W_ATT), BF16),
        scratch_shapes=[pltpu.VMEM((N_HEADS, HEAD_DIM, C), F32),
                        pltpu.VMEM((1, N_HEADS * C), F32)],
        compiler_params=pltpu.CompilerParams(
            dimension_semantics=("parallel", "arbitrary"), vmem_limit_bytes=VMEM_LIMIT),
        name="stickbreak",
    )(q, k, v, u)


def _route(scores, biased):
    neg_inf = -jnp.inf
    group_score = []
    for g in range(N_GROUPS):
        v = biased[g * GROUP:(g + 1) * GROUP]
        best = None
        for a in range(GROUP):
            for b in range(a + 1, GROUP):
                pair = v[a] + v[b]
                best = pair if best is None else jnp.maximum(best, pair)
        group_score.append(best)
    gmax = functools.reduce(jnp.maximum, group_score)
    taken = jnp.zeros_like(gmax) > 1.0
    in_group = []
    for g in range(N_GROUPS):
        sel = jnp.logical_and(group_score[g] == gmax, jnp.logical_not(taken))
        taken = jnp.logical_or(taken, sel)
        in_group.append(sel)
    masked = [jnp.where(in_group[e // GROUP], biased[e], neg_inf) for e in range(N_EXPERTS)]

    def first_argmax(vals):
        vmax = functools.reduce(jnp.maximum, vals)
        taken_ = jnp.zeros_like(vmax) > 1.0
        picks = []
        for v in vals:
            sel = jnp.logical_and(v == vmax, jnp.logical_not(taken_))
            taken_ = jnp.logical_or(taken_, sel)
            picks.append(sel)
        return picks

    pick1 = first_argmax(masked)
    masked2 = [jnp.where(pick1[e], neg_inf, masked[e]) for e in range(N_EXPERTS)]
    pick2 = first_argmax(masked2)
    w1 = functools.reduce(jnp.add, [jnp.where(pick1[e], scores[e], 0.0) for e in range(N_EXPERTS)])
    w2 = functools.reduce(jnp.add, [jnp.where(pick2[e], scores[e], 0.0) for e in range(N_EXPERTS)])
    tot = w1 + w2
    return [jnp.where(pick1[e], w1 / tot, 0.0) + jnp.where(pick2[e], w2 / tot, 0.0)
            for e in range(N_EXPERTS)]


def _split_bf16(v):
    hi = v.astype(BF16)
    return hi, (v - hi.astype(F32)).astype(BF16)


def _merge_kernel(x_ref, ya_ref, yb_ref, wa_ref, wb_ref, wg_ref, bg_ref, wo_ref, g1_ref, b1_ref,
                  wr_hi_ref, wr_lo_ref, rb_ref, x1_ref, comb_ref, *, d):
    x = x_ref[...]
    gates = jax.nn.sigmoid(_dot(x.astype(BF16), wg_ref[...]) + bg_ref[...])
    a = _dot(ya_ref[...], wa_ref[...])
    b = _dot(yb_ref[...], wb_ref[...])
    merged = gates[:, :d] * a + gates[:, d:] * b
    mix = _dot(merged.astype(BF16), wo_ref[...])
    x1 = _layer_norm_rows(DN_ALPHA * x + mix, g1_ref[...], b1_ref[...])
    x1_ref[...] = x1
    x_hi, x_lo = _split_bf16(x1)
    nt = lambda w_, x_: lax.dot_general(w_, x_, (((1,), (1,)), ((), ())), preferred_element_type=F32)
    logits = nt(wr_hi_ref[...], x_hi) + (nt(wr_hi_ref[...], x_lo) + nt(wr_lo_ref[...], x_hi))
    sc = jax.nn.sigmoid(logits)
    bs = sc + rb_ref[...]
    scores = [sc[e:e + 1] for e in range(N_EXPERTS)]
    biased = [bs[e:e + 1] for e in range(N_EXPERTS)]
    comb_ref[...] = jnp.concatenate(_route(scores, biased), axis=0)


def _merge(x2d, ya, yb, wa, wb, wg, bg, wo, g1, b1, wr_hi, wr_lo, rb, *, tm):
    n, d = x2d.shape
    full = lambda a: pl.BlockSpec(a.shape, lambda ti: (0,) * a.ndim)
    row = lambda cols: pl.BlockSpec((tm, cols), lambda ti: (ti, 0))
    return pl.pallas_call(
        functools.partial(_merge_kernel, d=d),
        grid=(n // tm,),
        in_specs=[row(d), row(W_ATT), row(W_ATT), full(wa), full(wb), full(wg), full(bg), full(wo),
                  full(g1), full(b1), full(wr_hi), full(wr_lo), full(rb)],
        out_specs=[row(d), pl.BlockSpec((N_EXPERTS, tm), lambda ti: (0, ti))],
        out_shape=[jax.ShapeDtypeStruct((n, d), F32), jax.ShapeDtypeStruct((N_EXPERTS, n), F32)],
        compiler_params=pltpu.CompilerParams(
            dimension_semantics=("parallel",), vmem_limit_bytes=VMEM_LIMIT),
        name="merge",
    )(x2d, ya, yb, wa, wb, wg, bg, wo, g1, b1, wr_hi, wr_lo, rb)


def _moe_kernel(x_ref, comb_ref, combt_ref, l_ref, wg_ref, wu_ref, wd_ref, g2_ref, b2_ref, out_ref,
                acc_s, xb_s, rank_s, *, tm):
    e = pl.program_id(1)

    @pl.when(e == 0)
    def _():
        acc_s[...] = jnp.zeros(acc_s.shape, F32)
        xb_s[...] = x_ref[...].astype(BF16)
        picked = jnp.where(comb_ref[...] > 0.0, 1.0, 0.0).astype(BF16)
        rank_s[...] = _dot(picked, l_ref[...])

    c_row = comb_ref[pl.ds(e, 1), :]
    slot_of = jnp.where(c_row > 0.0, rank_s[pl.ds(e, 1), :], -1.0)
    n_routed = jnp.sum(jnp.where(c_row > 0.0, 1.0, 0.0))
    n_blocks = ((n_routed + (MOE_SLOTS - 1)) * (1.0 / MOE_SLOTS)).astype(jnp.int32)
    lane = lax.broadcasted_iota(jnp.int32, combt_ref.shape, 1)
    c_col = jnp.sum(jnp.where(lane == e, combt_ref[...], 0.0), axis=-1, keepdims=True)
    slot_iota = lax.broadcasted_iota(jnp.int32, (MOE_SLOTS, tm), 0).astype(F32)

    pad_sel = jnp.zeros((MOE_SCATTER_K - MOE_SLOTS, tm), F32)
    pad_y = jnp.zeros((MOE_SCATTER_K - MOE_SLOTS, acc_s.shape[1]), BF16)

    def block(b, carry):
        sel = jnp.where(slot_of == slot_iota + (b * MOE_SLOTS).astype(F32), 1.0, 0.0)
        xs = _dot(sel.astype(BF16), xb_s[...]).astype(BF16)
        h = jax.nn.silu(_dot(xs, wg_ref[0])) * _dot(xs, wu_ref[0])
        y = _dot(h.astype(BF16), wd_ref[0]).astype(BF16)
        sel_t = jnp.concatenate([sel, pad_sel], axis=0).T.astype(BF16)
        acc_s[...] += c_col * _dot(sel_t, jnp.concatenate([y, pad_y], axis=0))
        return carry

    lax.fori_loop(0, n_blocks, block, 0)

    @pl.when(e == pl.num_programs(1) - 1)
    def _():
        out_ref[...] = _layer_norm_rows(DN_ALPHA * x_ref[...] + acc_s[...], g2_ref[...], b2_ref[...])


def _moe(x1, comb, comb_t, later, wg, wu, wd, g2, b2, *, tm):
    n, d = x1.shape
    ne, _, de = wg.shape
    full = lambda a: pl.BlockSpec(a.shape, lambda ti, e: (0,) * a.ndim)
    return pl.pallas_call(
        functools.partial(_moe_kernel, tm=tm),
        grid=(n // tm, ne),
        in_specs=[pl.BlockSpec((tm, d), lambda ti, e: (ti, 0)),
                  pl.BlockSpec((ne, tm), lambda ti, e: (0, ti)),
                  pl.BlockSpec((tm, ne), lambda ti, e: (ti, 0)),
                  full(later),
                  pl.BlockSpec((1, d, de), lambda ti, e: (e, 0, 0)),
                  pl.BlockSpec((1, d, de), lambda ti, e: (e, 0, 0)),
                  pl.BlockSpec((1, de, d), lambda ti, e: (e, 0, 0)),
                  full(g2), full(b2)],
        out_specs=pl.BlockSpec((tm, d), lambda ti, e: (ti, 0)),
        out_shape=jax.ShapeDtypeStruct((n, d), F32),
        scratch_shapes=[pltpu.VMEM((tm, d), F32), pltpu.VMEM((tm, d), BF16), pltpu.VMEM((ne, tm), F32)],
        compiler_params=pltpu.CompilerParams(
            dimension_semantics=("parallel", "arbitrary"), vmem_limit_bytes=VMEM_LIMIT),
        name="moe",
    )(x1, comb, comb_t, later, wg, wu, wd, g2, b2)


def _rope_tables(seq):
    inv_freq = ROPE_THETA ** (-jnp.arange(HALF, dtype=F32) / HALF)
    ang = inv_freq[:, None] * jnp.arange(seq, dtype=F32)[None, :]
    return jnp.cos(ang), jnp.sin(ang)


def _later_key_matrix():
    s = jnp.arange(C)[:, None]
    j = jnp.arange(C)[None, :]
    return jnp.concatenate([(j > s).astype(BF16), jnp.ones((16, C), BF16)], axis=0)


def _earlier_key_matrix():
    s = jnp.arange(C)[:, None]
    j = jnp.arange(C)[None, :]
    return jnp.concatenate([(j < s).astype(BF16), jnp.ones((16, C), BF16)], axis=0)


def kernel(x, w_in, b_gate, idx_k_norm_g, idx_k_norm_b, w_branch_a, w_branch_b, w_out, ln1_g, ln1_b,
           w_router, router_bias, exp_w_gate, exp_w_up, exp_w_down, ln2_g, ln2_b):
    bsz, seq, d = x.shape
    n = bsz * seq
    cos_t, sin_t = _rope_tables(seq)
    u = _later_key_matrix()
    t = _earlier_key_matrix()
    tok = jnp.arange(MOE_TILE)
    moe_later = (tok[:, None] < tok[None, :]).astype(BF16)
    wr_hi, wr_lo = _split_bf16(w_router.T)
    rb = router_bias.reshape(N_EXPERTS, 1)
    o_qi = 3 * W_ATT
    o_qb = o_qi + W_IDX + HEAD_DIM + IDX_HEADS
    o_g = o_qb + 3 * W_ATT
    for l in range(DEPTH):
        w = w_in[l]
        wp = _cast_bf16(jnp.concatenate(
            [w[:, :o_qi], w[:, o_qb:o_g], w[:, o_qi:o_qb], jnp.zeros((d, O_END - o_g), F32)], axis=1), rows=256)
        wg = _cast_bf16(w[:, o_g:], rows=256)
        qa, ka, va, qb, kb, vb, qi, ki, wi = _inproj(
            x, wp, cos_t, sin_t, idx_k_norm_g[l].reshape(HEAD_DIM, 1), idx_k_norm_b[l].reshape(HEAD_DIM, 1), tm=512)
        ya = _dsa(qi, ki, wi, qa, ka, va, t)
        yb = _sb(qb, kb, vb, u)
        x1, comb = _merge(
            x.reshape(n, d), ya.reshape(n, W_ATT), yb.reshape(n, W_ATT),
            w_branch_a[l].astype(BF16), w_branch_b[l].astype(BF16), wg,
            b_gate[l].reshape(1, 2 * d), w_out[l].astype(BF16), ln1_g[l].reshape(1, d), ln1_b[l].reshape(1, d),
            wr_hi, wr_lo, rb, tm=512)
        x = _moe(x1, comb, comb.T, moe_later, exp_w_gate[l].astype(BF16), exp_w_up[l].astype(BF16),
                 exp_w_down[l].astype(BF16), ln2_g[l].reshape(1, d), ln2_b[l].reshape(1, d),
                 tm=MOE_TILE).reshape(bsz, seq, d)
    return x
```

```python
import functools
import math

import jax
import jax.numpy as jnp
from jax import lax
from jax.experimental import pallas as pl
from jax.experimental.pallas import tpu as pltpu

F32 = jnp.float32
BF16 = jnp.bfloat16

LANES = 128
SUBLANES = 8
C = 256
HEAD_DIM = 64
HALF = HEAD_DIM // 2
N_HEADS = 8
N_PAIRS = N_HEADS // 2
W_ATT = N_HEADS * HEAD_DIM
IDX_HEADS = 4
W_IDX = IDX_HEADS * HEAD_DIM
N_SEL_MAX = 256
ROPE_THETA = 10000.0
LN_EPS = 1e-5
N_EXPERTS = 16
N_GROUPS = 4
GROUP = N_EXPERTS // N_GROUPS
DEPTH = 2
DN_ALPHA = (2 * DEPTH) ** 0.25
INT_MIN = -(2 ** 31)
INT_MAX = 2 ** 31 - 1
VALUE_BISECT_ROUNDS = 12
FLT_MIN_BITS = 0x00800000
CODE_INF = 0x7F800000 - (FLT_MIN_BITS - 1)
CODE_LOWEST = -(0x7F7FFFFF - (FLT_MIN_BITS - 1))
NEG_BIG = -1e30
MOE_TILE = 1024
MOE_SLOTS = 160
MOE_SCATTER_K = 256
BISECT_UNROLL = 4
V_EXT = HEAD_DIM + 16
VMEM_LIMIT = 56 * 1024 * 1024

O_QA, O_KA, O_VA, O_QB, O_KB, O_VB = (i * W_ATT for i in range(6))
O_IDX = 6 * W_ATT
W_IDX_PAD = 384
O_END = O_IDX + W_IDX_PAD


def _dot(a, b):
    return jnp.dot(a, b, preferred_element_type=F32)


def _layer_norm_rows(v, g, b):
    mu = jnp.mean(v, axis=-1, keepdims=True)
    d = v - mu
    var = jnp.mean(d * d, axis=-1, keepdims=True)
    return d * lax.rsqrt(var + LN_EPS) * g + b


def _reduce_rows(x, op):
    tiles = [x[r:r + SUBLANES] for r in range(0, x.shape[0], SUBLANES)]
    while len(tiles) > 1:
        nxt = [op(tiles[t], tiles[t + 1]) for t in range(0, len(tiles) - 1, 2)]
        if len(tiles) % 2:
            nxt.append(tiles[-1])
        tiles = nxt
    red = jnp.sum if op is jnp.add else jnp.max
    return red(tiles[0], axis=0, keepdims=True)


def _chunk_pairs(n, body, init):
    def pair(it, carry):
        return body(2 * it + 1, body(2 * it, carry))
    carry = lax.fori_loop(0, n // 2, pair, init)
    return lax.cond(n % 2 == 1, lambda c: body(n - 1, c), lambda c: c, carry)


def _head_rows(h):
    return slice(h * HEAD_DIM, (h + 1) * HEAD_DIM)


def _pair_queries(q_t):
    zero = jnp.zeros((HEAD_DIM, C), q_t.dtype)
    blocks = []
    for p in range(N_PAIRS):
        top = jnp.concatenate([q_t[_head_rows(2 * p)], zero], axis=1)
        bot = jnp.concatenate([zero, q_t[_head_rows(2 * p + 1)]], axis=1)
        blocks.append(jnp.concatenate([top, bot], axis=0))
    return blocks


def _cast_kernel(src_ref, dst_ref):
    dst_ref[...] = src_ref[...].astype(dst_ref.dtype)


def _cast_bf16(w, *, rows):
    r, c = w.shape
    return pl.pallas_call(
        _cast_kernel,
        grid=(r // rows,),
        in_specs=[pl.BlockSpec((rows, c), lambda i: (i, 0))],
        out_specs=pl.BlockSpec((rows, c), lambda i: (i, 0)),
        out_shape=jax.ShapeDtypeStruct((r, c), BF16),
        compiler_params=pltpu.CompilerParams(dimension_semantics=("parallel",)),
        name="castw",
    )(w)


def _inproj_kernel(x_ref, w_ref, cos_ref, sin_ref, g_ref, b_ref,
                   qa_ref, ka_ref, va_ref, qb_ref, kb_ref, vb_ref, qi_ref, ki_ref, wi_ref, *, tm):
    xb = x_ref[0].astype(BF16)
    cos = cos_ref[...]
    sin = sin_ref[...]

    def proj(c0, c1):
        return _dot(xb, w_ref[:, c0:c1])

    def rope(p, scale):
        outs = []
        for h in range(p.shape[0] // HEAD_DIM):
            x1 = p[h * HEAD_DIM:h * HEAD_DIM + HALF]
            x2 = p[h * HEAD_DIM + HALF:(h + 1) * HEAD_DIM]
            outs.append((x1 * cos - x2 * sin) * scale)
            outs.append((x2 * cos + x1 * sin) * scale)
        return jnp.concatenate(outs, axis=0)

    def store_fm(ref, val):
        for c in range(tm // C):
            ref[0, c] = val[:, c * C:(c + 1) * C].astype(ref.dtype)

    qscale = math.log2(math.e) / math.sqrt(HEAD_DIM)
    store_fm(qa_ref, rope(proj(O_QA, O_KA).T, qscale))
    ka_ref[0] = rope(proj(O_KA, O_VA).T, 1.0).T.astype(ka_ref.dtype)
    store_fm(va_ref, proj(O_VA, O_QB).T)
    store_fm(qb_ref, proj(O_QB, O_KB).T * qscale)
    kb_ref[0] = proj(O_KB, O_VB).astype(kb_ref.dtype)
    store_fm(vb_ref, proj(O_VB, O_IDX).T)
    idx = proj(O_IDX, O_END).T
    store_fm(qi_ref, rope(idx[:W_IDX], 1.0))
    ki = idx[W_IDX:W_IDX + HEAD_DIM]
    mu = jnp.mean(ki, axis=0, keepdims=True)
    d = ki - mu
    var = jnp.mean(d * d, axis=0, keepdims=True)
    ki = rope(d * lax.rsqrt(var + LN_EPS) * g_ref[...] + b_ref[...], 1.0)
    ki_ref[0] = jnp.concatenate([ki, jnp.zeros_like(ki)], axis=0).T.astype(ki_ref.dtype)
    w_scale = IDX_HEADS ** -0.5 * HEAD_DIM ** -0.5
    store_fm(wi_ref, idx[W_IDX + HEAD_DIM:W_IDX + HEAD_DIM + SUBLANES] * w_scale)


def _inproj(x, w, cos_t, sin_t, g, b, *, tm):
    bsz, seq, d = x.shape
    nch = seq // C
    fm = lambda rows, dt: jax.ShapeDtypeStruct((bsz, nch, rows, C), dt)
    fm_spec = lambda rows: pl.BlockSpec((1, tm // C, rows, C), lambda bi, ti: (bi, ti, 0, 0))
    tok = lambda cols: jax.ShapeDtypeStruct((bsz, seq, cols), BF16)
    tok_spec = lambda cols: pl.BlockSpec((1, tm, cols), lambda bi, ti: (bi, ti, 0))
    full = lambda a: pl.BlockSpec(a.shape, lambda bi, ti: (0,) * a.ndim)
    return pl.pallas_call(
        functools.partial(_inproj_kernel, tm=tm),
        grid=(bsz, seq // tm),
        in_specs=[
            tok_spec(d), full(w),
            pl.BlockSpec((HALF, tm), lambda bi, ti: (0, ti)),
            pl.BlockSpec((HALF, tm), lambda bi, ti: (0, ti)),
            full(g), full(b),
        ],
        out_specs=[fm_spec(W_ATT), tok_spec(W_ATT), fm_spec(W_ATT), fm_spec(W_ATT), tok_spec(W_ATT), fm_spec(W_ATT),
                   fm_spec(W_IDX), tok_spec(LANES), fm_spec(SUBLANES)],
        out_shape=[fm(W_ATT, BF16), tok(W_ATT), fm(W_ATT, BF16), fm(W_ATT, BF16), tok(W_ATT), fm(W_ATT, BF16),
                   fm(W_IDX, BF16), tok(LANES), fm(SUBLANES, F32)],
        compiler_params=pltpu.CompilerParams(
            dimension_semantics=("parallel", "parallel"), vmem_limit_bytes=VMEM_LIMIT),
        name="inproj",
    )(x, w, cos_t, sin_t, g, b)


def _dsa_kernel(qi_ref, ki_ref, wi_ref, qa_ref, ka_ref, va_ref, t_ref, y_ref,
                key_s, bias_s, m_s, acc_s, *, n_sel):
    i = pl.program_id(1)
    nvis = i + 1
    krow = lax.broadcasted_iota(jnp.int32, (C, C), 0)
    qcol = lax.broadcasted_iota(jnp.int32, (C, C), 1)
    qlane = lax.broadcasted_iota(jnp.int32, (1, C), 1)

    qi_t = qi_ref[0, 0]
    qi_top = jnp.concatenate([qi_t[_head_rows(h)] for h in range(IDX_HEADS)], axis=1)
    qi_rhs = jnp.concatenate([qi_top, jnp.zeros_like(qi_top)], axis=0)
    wi = wi_ref[0, 0]

    def tile_tree(x, op):
        tiles = [x[r:r + SUBLANES] for r in range(0, C, SUBLANES)]
        while len(tiles) > 1:
            tiles = [op(tiles[t], tiles[t + 1]) for t in range(0, len(tiles), 2)]
        return tiles[0]

    def score_chunk(j, carry):
        lg = _dot(ki_ref[0, pl.ds(pl.multiple_of(j * C, C), C), :], qi_rhs)
        sc = jnp.maximum(lg[:, :C], 0.0) * wi[0:1]
        for h in range(1, IDX_HEADS):
            sc = sc + jnp.maximum(lg[:, h * C:(h + 1) * C], 0.0) * wi[h:h + 1]
        visible = (j - i) * C + krow <= qcol
        key_s[j] = jnp.where(visible, sc, -jnp.inf)
        return (jnp.minimum(carry[0], tile_tree(jnp.where(visible, sc, jnp.inf), jnp.minimum)),
                jnp.maximum(carry[1], tile_tree(jnp.where(visible, sc, -jnp.inf), jnp.maximum)))

    smin, smax = _chunk_pairs(nvis, score_chunk, (jnp.full((SUBLANES, C), jnp.inf, F32),
                                                  jnp.full((SUBLANES, C), -jnp.inf, F32)))
    smin = jnp.min(smin, axis=0, keepdims=True)
    smax = jnp.max(smax, axis=0, keepdims=True)

    def count(pred):
        def body(j, acc):
            return acc + tile_tree(pred(j * C, key_s[j]), jnp.add)
        acc = _chunk_pairs(nvis, body, jnp.zeros((SUBLANES, C), F32))
        return jnp.sum(acc, axis=0, keepdims=True)

    def code_to_float(code):
        mag = jnp.abs(code) + (FLT_MIN_BITS - 1)
        bits = jnp.where(code > 0, mag, jnp.where(code < 0, mag | INT_MIN, 0))
        return lax.bitcast_convert_type(bits, F32)

    def float_to_code(v):
        bits = lax.bitcast_convert_type(v, jnp.int32)
        mag = jnp.maximum((bits & INT_MAX) - (FLT_MIN_BITS - 1), 0)
        return jnp.where(bits < 0, -mag, mag)

    n_vis_q = (i * C + qlane + 1).astype(F32)
    few = n_vis_q <= n_sel
    lo0 = jnp.where(few, CODE_LOWEST, float_to_code(smin))
    hi0 = jnp.where(few, CODE_LOWEST + 1, float_to_code(smax) + 1)
    n_active0 = jnp.sum(jnp.where(hi0 - 1 > lo0, 1.0, 0.0))

    def bisect_cond(st):
        return st[4] > 0.5

    def bisect_step(st, by_value, probe=None):
        lo, hi, c_lo, c_hi = st[:4]
        mid_v = float_to_code(0.5 * code_to_float(lo) + 0.5 * code_to_float(hi))
        mid_c = (lo >> 1) + (hi >> 1) + (lo & hi & 1)
        mid = jnp.maximum(lo + 1, jnp.minimum(jnp.where(by_value, mid_v, mid_c), hi - 1))
        if probe is not None:
            mid = jnp.where(jnp.logical_and(lo < probe, probe < hi), probe, mid)
        mid_f = code_to_float(mid)
        c = count(lambda pos, k: jnp.where(k >= mid_f, 1.0, 0.0))
        ge = c >= n_sel
        exact = c == n_sel
        lo_n = jnp.where(ge, mid, lo)
        hi_n = jnp.where(exact, mid + 1, jnp.where(ge, hi, mid))
        return (lo_n, hi_n, jnp.where(ge, c, c_lo), jnp.where(ge, c_hi, c)) + tuple(st[4:])

    def bisect_key(st):
        rounds = st[5]
        for _ in range(BISECT_UNROLL):
            st = bisect_step(st, rounds < VALUE_BISECT_ROUNDS)
        lo, hi = st[0], st[1]
        return st[:4] + (jnp.sum(jnp.where(hi - 1 > lo, 1.0, 0.0)), rounds + 1)

    st0 = (lo0, hi0, n_vis_q, jnp.zeros((1, C), F32), n_active0, jnp.int32(0))
    st0 = bisect_step(bisect_step(st0, True, probe=0), True, probe=1)
    thr_code, _, c_thr, c_above, _, _ = lax.while_loop(bisect_cond, bisect_key, st0)
    thr = code_to_float(thr_code)

    need = jnp.where(c_thr > n_sel, n_sel - c_above, float(n_sel))

    def bias_chunk(j, ties_before):
        k = key_s[j]
        tie = jnp.where(k == thr, 1.0, 0.0).astype(BF16)
        sums = _dot(t_ref[...], tie)
        keep = jnp.where(ties_before + sums[:C] < need, 0.0, NEG_BIG)
        bias_s[j] = jnp.where(k > thr, 0.0, jnp.where(k == thr, keep, NEG_BIG))
        return ties_before + sums[C:C + 1]

    _chunk_pairs(nvis, bias_chunk, jnp.zeros((1, C), F32))

    q_blocks = _pair_queries(qa_ref[0, 0])
    m_s[...] = jnp.full(m_s.shape, NEG_BIG, F32)
    acc_s[...] = jnp.zeros(acc_s.shape, F32)
    ones_rows = jnp.ones((V_EXT - HEAD_DIM, C), BF16)

    def attend(j, carry):
        k_c = ka_ref[0, pl.ds(pl.multiple_of(j * C, C), C), :]
        s = jnp.concatenate([_dot(k_c[:, p * LANES:(p + 1) * LANES], q_blocks[p]) for p in range(N_PAIRS)], axis=1)
        s = s + jnp.concatenate([bias_s[j]] * N_HEADS, axis=1)
        m_old = m_s[...]
        m_new = jnp.maximum(m_old, _reduce_rows(s, jnp.maximum))
        alpha = jnp.exp2(m_old - m_new)
        pr = jnp.exp2(s - m_new).astype(BF16)
        m_s[...] = m_new
        for h in range(N_HEADS):
            v_ext = jnp.concatenate([va_ref[0, j, _head_rows(h), :], ones_rows], axis=0)
            acc_s[h] = acc_s[h] * alpha[:, h * C:(h + 1) * C] + _dot(v_ext, pr[:, h * C:(h + 1) * C])
        return carry

    _chunk_pairs(nvis, attend, 0)
    outs = []
    for h in range(N_HEADS):
        a = acc_s[h]
        outs.append(a[:HEAD_DIM] / a[HEAD_DIM:HEAD_DIM + 1])
    y_ref[0] = jnp.concatenate(outs, axis=0).T.astype(y_ref.dtype)


def _dsa(qi, ki, wi, qa, ka, va, t):
    bsz, nch, _, _ = qa.shape
    seq = nch * C
    n_sel = min(N_SEL_MAX, seq // 4)
    qspec = lambda rows: pl.BlockSpec((1, 1, rows, C), lambda bi, qi_: (bi, qi_, 0, 0))
    return pl.pallas_call(
        functools.partial(_dsa_kernel, n_sel=n_sel),
        grid=(bsz, nch),
        in_specs=[qspec(W_IDX),
                  pl.BlockSpec((1, seq, LANES), lambda bi, qi_: (bi, 0, 0)),
                  qspec(SUBLANES), qspec(W_ATT),
                  pl.BlockSpec((1, seq, W_ATT), lambda bi, qi_: (bi, 0, 0)),
                  pl.BlockSpec((1, nch, W_ATT, C), lambda bi, qi_: (bi, 0, 0, 0)),
                  pl.BlockSpec(t.shape, lambda bi, qi_: (0, 0))],
        out_specs=pl.BlockSpec((1, C, W_ATT), lambda bi, qi_: (bi, qi_, 0)),
        out_shape=jax.ShapeDtypeStruct((bsz, seq, W_ATT), BF16),
        scratch_shapes=[
            pltpu.VMEM((nch, C, C), F32),
            pltpu.VMEM((nch, C, C), F32),
            pltpu.VMEM((1, N_HEADS * C), F32),
            pltpu.VMEM((N_HEADS, V_EXT, C), F32),
        ],
        compiler_params=pltpu.CompilerParams(
            dimension_semantics=("parallel", "arbitrary"), vmem_limit_bytes=VMEM_LIMIT),
        name="dsa",
    )(qi, ki, wi, qa, ka, va, t)


def _sb_kernel(q_ref, k_ref, v_ref, u_ref, y_ref, acc_s, carry_s):
    i = pl.program_id(1)
    q_blocks = _pair_queries(q_ref[0, 0])
    acc_s[...] = jnp.zeros(acc_s.shape, F32)
    carry_s[...] = jnp.zeros(carry_s.shape, F32)

    def step(j, diagonal):
        k_c = k_ref[0, pl.ds(pl.multiple_of(j * C, C), C), :]
        z = jnp.concatenate([_dot(k_c[:, p * LANES:(p + 1) * LANES], q_blocks[p]) for p in range(N_PAIRS)], axis=1)
        sp = jnp.log2(1.0 + jnp.exp2(-jnp.abs(z)))
        log_beta = jnp.minimum(z, 0.0) - sp
        log_rest = log_beta - z
        if diagonal:
            before = (lax.broadcasted_iota(jnp.int32, z.shape, 0)
                      < (lax.broadcasted_iota(jnp.int32, z.shape, 1) & (C - 1)))
            log_rest = jnp.where(before, log_rest, 0.0)
        sums = _dot(u_ref[...], log_rest.astype(BF16))
        carry = carry_s[...]
        att = jnp.exp2(log_beta + (sums[:C] + carry))
        if diagonal:
            att = jnp.where(before, att, 0.0)
        carry_s[...] = carry + sums[C:C + 1]
        att = att.astype(BF16)
        for h in range(N_HEADS):
            acc_s[h] += _dot(v_ref[0, j, _head_rows(h), :], att[:, h * C:(h + 1) * C])

    step(i, True)

    def earlier(it, carry):
        step(i - 1 - it, False)
        return carry

    _chunk_pairs(i, earlier, 0)
    y_ref[0] = jnp.concatenate([acc_s[h] for h in range(N_HEADS)], axis=0).T.astype(y_ref.dtype)


def _sb(q, k, v, u):
    bsz, nch, _, _ = q.shape
    seq = nch * C
    return pl.pallas_call(
        _sb_kernel,
        grid=(bsz, nch),
        in_specs=[pl.BlockSpec((1, 1, W_ATT, C), lambda bi, qi_: (bi, qi_, 0, 0)),
                  pl.BlockSpec((1, seq, W_ATT), lambda bi, qi_: (bi, 0, 0)),
                  pl.BlockSpec((1, nch, W_ATT, C), lambda bi, qi_: (bi, 0, 0, 0)),
                  pl.BlockSpec(u.shape, lambda bi, qi_: (0, 0))],
        out_specs=pl.BlockSpec((1, C, W_ATT), lambda bi, qi_: (bi, qi_, 0)),
        out_shape=jax.ShapeDtypeStruct((bsz, seq, W_ATT), BF16),
        scratch_shapes=[pltpu.VMEM((N_HEADS, HEAD_DIM, C), F32),
                        pltpu.VMEM((1, N_HEADS * C), F32)],
        compiler_params=pltpu.CompilerParams(
            dimension_semantics=("parallel", "arbitrary"), vmem_limit_bytes=VMEM_LIMIT),
        name="stickbreak",
    )(q, k, v, u)


def _route(scores, biased):
    neg_inf = -jnp.inf
    group_score = []
    for g in range(N_GROUPS):
        v = biased[g * GROUP:(g + 1) * GROUP]
        best = None
        for a in range(GROUP):
            for b in range(a + 1, GROUP):
                pair = v[a] + v[b]
                best = pair if best is None else jnp.maximum(best, pair)
        group_score.append(best)
    gmax = functools.reduce(jnp.maximum, group_score)
    taken = jnp.zeros_like(gmax) > 1.0
    in_group = []
    for g in range(N_GROUPS):
        sel = jnp.logical_and(group_score[g] == gmax, jnp.logical_not(taken))
        taken = jnp.logical_or(taken, sel)
        in_group.append(sel)
    masked = [jnp.where(in_group[e // GROUP], biased[e], neg_inf) for e in range(N_EXPERTS)]

    def first_argmax(vals):
        vmax = functools.reduce(jnp.maximum, vals)
        taken_ = jnp.zeros_like(vmax) > 1.0
        picks = []
        for v in vals:
            sel = jnp.logical_and(v == vmax, jnp.logical_not(taken_))
            taken_ = jnp.logical_or(taken_, sel)
            picks.append(sel)
        return picks

    pick1 = first_argmax(masked)
    masked2 = [jnp.where(pick1[e], neg_inf, masked[e]) for e in range(N_EXPERTS)]
    pick2 = first_argmax(masked2)
    w1 = functools.reduce(jnp.add, [jnp.where(pick1[e], scores[e], 0.0) for e in range(N_EXPERTS)])
    w2 = functools.reduce(jnp.add, [jnp.where(pick2[e], scores[e], 0.0) for e in range(N_EXPERTS)])
    tot = w1 + w2
    return [jnp.where(pick1[e], w1 / tot, 0.0) + jnp.where(pick2[e], w2 / tot, 0.0)
            for e in range(N_EXPERTS)]


def _split_bf16(v):
    hi = v.astype(BF16)
    return hi, (v - hi.astype(F32)).astype(BF16)


def _merge_kernel(x_ref, ya_ref, yb_ref, wa_ref, wb_ref, wg_ref, bg_ref, wo_ref, g1_ref, b1_ref,
                  wr_hi_ref, wr_lo_ref, rb_ref, x1_ref, comb_ref, *, d):
    x = x_ref[...]
    gates = jax.nn.sigmoid(_dot(x.astype(BF16), wg_ref[...]) + bg_ref[...])
    a = _dot(ya_ref[...], wa_ref[...])
    b = _dot(yb_ref[...], wb_ref[...])
    merged = gates[:, :d] * a + gates[:, d:] * b
    mix = _dot(merged.astype(BF16), wo_ref[...])
    x1 = _layer_norm_rows(DN_ALPHA * x + mix, g1_ref[...], b1_ref[...])
    x1_ref[...] = x1
    x_hi, x_lo = _split_bf16(x1)
    nt = lambda w_, x_: lax.dot_general(w_, x_, (((1,), (1,)), ((), ())), preferred_element_type=F32)
    logits = nt(wr_hi_ref[...], x_hi) + (nt(wr_hi_ref[...], x_lo) + nt(wr_lo_ref[...], x_hi))
    sc = jax.nn.sigmoid(logits)
    bs = sc + rb_ref[...]
    scores = [sc[e:e + 1] for e in range(N_EXPERTS)]
    biased = [bs[e:e + 1] for e in range(N_EXPERTS)]
    comb_ref[...] = jnp.concatenate(_route(scores, biased), axis=0)


def _merge(x2d, ya, yb, wa, wb, wg, bg, wo, g1, b1, wr_hi, wr_lo, rb, *, tm):
    n, d = x2d.shape
    full = lambda a: pl.BlockSpec(a.shape, lambda ti: (0,) * a.ndim)
    row = lambda cols: pl.BlockSpec((tm, cols), lambda ti: (ti, 0))
    return pl.pallas_call(
        functools.partial(_merge_kernel, d=d),
        grid=(n // tm,),
        in_specs=[row(d), row(W_ATT), row(W_ATT), full(wa), full(wb), full(wg), full(bg), full(wo),
                  full(g1), full(b1), full(wr_hi), full(wr_lo), full(rb)],
        out_specs=[row(d), pl.BlockSpec((N_EXPERTS, tm), lambda ti: (0, ti))],
        out_shape=[jax.ShapeDtypeStruct((n, d), F32), jax.ShapeDtypeStruct((N_EXPERTS, n), F32)],
        compiler_params=pltpu.CompilerParams(
            dimension_semantics=("parallel",), vmem_limit_bytes=VMEM_LIMIT),
        name="merge",
    )(x2d, ya, yb, wa, wb, wg, bg, wo, g1, b1, wr_hi, wr_lo, rb)


def _moe_kernel(x_ref, comb_ref, combt_ref, l_ref, wg_ref, wu_ref, wd_ref, g2_ref, b2_ref, out_ref,
                acc_s, xb_s, rank_s, *, tm):
    e = pl.program_id(1)

    @pl.when(e == 0)
    def _():
        acc_s[...] = jnp.zeros(acc_s.shape, F32)
        xb_s[...] = x_ref[...].astype(BF16)
        picked = jnp.where(comb_ref[...] > 0.0, 1.0, 0.0).astype(BF16)
        rank_s[...] = _dot(picked, l_ref[...])

    c_row = comb_ref[pl.ds(e, 1), :]
    slot_of = jnp.where(c_row > 0.0, rank_s[pl.ds(e, 1), :], -1.0)
    n_routed = jnp.sum(jnp.where(c_row > 0.0, 1.0, 0.0))
    n_blocks = ((n_routed + (MOE_SLOTS - 1)) * (1.0 / MOE_SLOTS)).astype(jnp.int32)
    lane = lax.broadcasted_iota(jnp.int32, combt_ref.shape, 1)
    c_col = jnp.sum(jnp.where(lane == e, combt_ref[...], 0.0), axis=-1, keepdims=True)
    slot_iota = lax.broadcasted_iota(jnp.int32, (MOE_SLOTS, tm), 0).astype(F32)

    pad_sel = jnp.zeros((MOE_SCATTER_K - MOE_SLOTS, tm), F32)
    pad_y = jnp.zeros((MOE_SCATTER_K - MOE_SLOTS, acc_s.shape[1]), BF16)

    def block(b, carry):
        sel = jnp.where(slot_of == slot_iota + (b * MOE_SLOTS).astype(F32), 1.0, 0.0)
        xs = _dot(sel.astype(BF16), xb_s[...]).astype(BF16)
        h = jax.nn.silu(_dot(xs, wg_ref[0])) * _dot(xs, wu_ref[0])
        y = _dot(h.astype(BF16), wd_ref[0]).astype(BF16)
        sel_t = jnp.concatenate([sel, pad_sel], axis=0).T.astype(BF16)
        acc_s[...] += c_col * _dot(sel_t, jnp.concatenate([y, pad_y], axis=0))
        return carry

    lax.fori_loop(0, n_blocks, block, 0)

    @pl.when(e == pl.num_programs(1) - 1)
    def _():
        out_ref[...] = _layer_norm_rows(DN_ALPHA * x_ref[...] + acc_s[...], g2_ref[...], b2_ref[...])


def _moe(x1, comb, comb_t, later, wg, wu, wd, g2, b2, *, tm):
    n, d = x1.shape
    ne, _, de = wg.shape
    full = lambda a: pl.BlockSpec(a.shape, lambda ti, e: (0,) * a.ndim)
    return pl.pallas_call(
        functools.partial(_moe_kernel, tm=tm),
        grid=(n // tm, ne),
        in_specs=[pl.BlockSpec((tm, d), lambda ti, e: (ti, 0)),
                  pl.BlockSpec((ne, tm), lambda ti, e: (0, ti)),
                  pl.BlockSpec((tm, ne), lambda ti, e: (ti, 0)),
                  full(later),
                  pl.BlockSpec((1, d, de), lambda ti, e: (e, 0, 0)),
                  pl.BlockSpec((1, d, de), lambda ti, e: (e, 0, 0)),
                  pl.BlockSpec((1, de, d), lambda ti, e: (e, 0, 0)),
                  full(g2), full(b2)],
        out_specs=pl.BlockSpec((tm, d), lambda ti, e: (ti, 0)),
        out_shape=jax.ShapeDtypeStruct((n, d), F32),
        scratch_shapes=[pltpu.VMEM((tm, d), F32), pltpu.VMEM((tm, d), BF16), pltpu.VMEM((ne, tm), F32)],
        compiler_params=pltpu.CompilerParams(
            dimension_semantics=("parallel", "arbitrary"), vmem_limit_bytes=VMEM_LIMIT),
        name="moe",
    )(x1, comb, comb_t, later, wg, wu, wd, g2, b2)


def _rope_tables(seq):
    inv_freq = ROPE_THETA ** (-jnp.arange(HALF, dtype=F32) / HALF)
    ang = inv_freq[:, None] * jnp.arange(seq, dtype=F32)[None, :]
    return jnp.cos(ang), jnp.sin(ang)


def _later_key_matrix():
    s = jnp.arange(C)[:, None]
    j = jnp.arange(C)[None, :]
    return jnp.concatenate([(j > s).astype(BF16), jnp.ones((16, C), BF16)], axis=0)


def _earlier_key_matrix():
    s = jnp.arange(C)[:, None]
    j = jnp.arange(C)[None, :]
    return jnp.concatenate([(j < s).astype(BF16), jnp.ones((16, C), BF16)], axis=0)


def kernel(x, w_in, b_gate, idx_k_norm_g, idx_k_norm_b, w_branch_a, w_branch_b, w_out, ln1_g, ln1_b,
           w_router, router_bias, exp_w_gate, exp_w_up, exp_w_down, ln2_g, ln2_b):
    bsz, seq, d = x.shape
    n = bsz * seq
    cos_t, sin_t = _rope_tables(seq)
    u = _later_key_matrix()
    t = _earlier_key_matrix()
    tok = jnp.arange(MOE_TILE)
    moe_later = (tok[:, None] < tok[None, :]).astype(BF16)
    wr_hi, wr_lo = _split_bf16(w_router.T)
    rb = router_bias.reshape(N_EXPERTS, 1)
    o_qi = 3 * W_ATT
    o_qb = o_qi + W_IDX + HEAD_DIM + IDX_HEADS
    o_g = o_qb + 3 * W_ATT
    for l in range(DEPTH):
        w = w_in[l]
        wp = _cast_bf16(jnp.concatenate(
            [w[:, :o_qi], w[:, o_qb:o_g], w[:, o_qi:o_qb], jnp.zeros((d, O_END - o_g), F32)], axis=1), rows=256)
        wg = _cast_bf16(w[:, o_g:], rows=256)
        qa, ka, va, qb, kb, vb, qi, ki, wi = _inproj(
            x, wp, cos_t, sin_t, idx_k_norm_g[l].reshape(HEAD_DIM, 1), idx_k_norm_b[l].reshape(HEAD_DIM, 1), tm=512)
        ya = _dsa(qi, ki, wi, qa, ka, va, t)
        yb = _sb(qb, kb, vb, u)
        x1, comb = _merge(
            x.reshape(n, d), ya.reshape(n, W_ATT), yb.reshape(n, W_ATT),
            w_branch_a[l].astype(BF16), w_branch_b[l].astype(BF16), wg,
            b_gate[l].reshape(1, 2 * d), w_out[l].astype(BF16), ln1_g[l].reshape(1, d), ln1_b[l].reshape(1, d),
            wr_hi, wr_lo, rb, tm=512)
        x = _moe(x1, comb, comb.T, moe_later, exp_w_gate[l].astype(BF16), exp_w_up[l].astype(BF16),
                 exp_w_down[l].astype(BF16), ln2_g[l].reshape(1, d), ln2_b[l].reshape(1, d),
                 tm=MOE_TILE).reshape(bsz, seq, d)
    return x
```

```python
import functools
import math

import jax
import jax.numpy as jnp
from jax import lax
from jax.experimental import pallas as pl
from jax.experimental.pallas import tpu as pltpu

F32 = jnp.float32
BF16 = jnp.bfloat16

LANES = 128
SUBLANES = 8
C = 256
HEAD_DIM = 64
HALF = HEAD_DIM // 2
N_HEADS = 8
N_PAIRS = N_HEADS // 2
W_ATT = N_HEADS * HEAD_DIM
IDX_HEADS = 4
W_IDX = IDX_HEADS * HEAD_DIM
N_SEL_MAX = 256
ROPE_THETA = 10000.0
LN_EPS = 1e-5
N_EXPERTS = 16
N_GROUPS = 4
GROUP = N_EXPERTS // N_GROUPS
DEPTH = 2
DN_ALPHA = (2 * DEPTH) ** 0.25
INT_MIN = -(2 ** 31)
INT_MAX = 2 ** 31 - 1
VALUE_BISECT_ROUNDS = 12
FLT_MIN_BITS = 0x00800000
CODE_INF = 0x7F800000 - (FLT_MIN_BITS - 1)
CODE_LOWEST = -(0x7F7FFFFF - (FLT_MIN_BITS - 1))
NEG_BIG = -1e30
MOE_TILE = 1024
MOE_SLOTS = 160
MOE_EXPERTS_PER_STEP = 2
MOE_SCATTER_K = 256
BISECT_UNROLL = 4
V_EXT = HEAD_DIM + 16
VMEM_LIMIT = 56 * 1024 * 1024

O_QA, O_KA, O_VA, O_QB, O_KB, O_VB = (i * W_ATT for i in range(6))
O_IDX = 6 * W_ATT
W_IDX_PAD = 384
O_END = O_IDX + W_IDX_PAD


def _dot(a, b):
    return jnp.dot(a, b, preferred_element_type=F32)


def _layer_norm_rows(v, g, b):
    mu = jnp.mean(v, axis=-1, keepdims=True)
    d = v - mu
    var = jnp.mean(d * d, axis=-1, keepdims=True)
    return d * lax.rsqrt(var + LN_EPS) * g + b


def _reduce_rows(x, op):
    tiles = [x[r:r + SUBLANES] for r in range(0, x.shape[0], SUBLANES)]
    while len(tiles) > 1:
        nxt = [op(tiles[t], tiles[t + 1]) for t in range(0, len(tiles) - 1, 2)]
        if len(tiles) % 2:
            nxt.append(tiles[-1])
        tiles = nxt
    red = jnp.sum if op is jnp.add else jnp.max
    return red(tiles[0], axis=0, keepdims=True)


def _chunk_pairs(n, body, init):
    def pair(it, carry):
        return body(2 * it + 1, body(2 * it, carry))
    carry = lax.fori_loop(0, n // 2, pair, init)
    return lax.cond(n % 2 == 1, lambda c: body(n - 1, c), lambda c: c, carry)


def _head_rows(h):
    return slice(h * HEAD_DIM, (h + 1) * HEAD_DIM)


def _pair_queries(q_t):
    zero = jnp.zeros((HEAD_DIM, C), q_t.dtype)
    blocks = []
    for p in range(N_PAIRS):
        top = jnp.concatenate([q_t[_head_rows(2 * p)], zero], axis=1)
        bot = jnp.concatenate([zero, q_t[_head_rows(2 * p + 1)]], axis=1)
        blocks.append(jnp.concatenate([top, bot], axis=0))
    return blocks


def _cast_kernel(src_ref, dst_ref):
    dst_ref[...] = src_ref[...].astype(dst_ref.dtype)


def _cast_bf16(w, *, rows):
    r, c = w.shape
    return pl.pallas_call(
        _cast_kernel,
        grid=(r // rows,),
        in_specs=[pl.BlockSpec((rows, c), lambda i: (i, 0))],
        out_specs=pl.BlockSpec((rows, c), lambda i: (i, 0)),
        out_shape=jax.ShapeDtypeStruct((r, c), BF16),
        compiler_params=pltpu.CompilerParams(dimension_semantics=("parallel",)),
        name="castw",
    )(w)


def _inproj_kernel(x_ref, w_ref, cos_ref, sin_ref, g_ref, b_ref,
                   qa_ref, ka_ref, va_ref, qb_ref, kb_ref, vb_ref, qi_ref, ki_ref, wi_ref, *, tm):
    xb = x_ref[0].astype(BF16)
    cos = cos_ref[...]
    sin = sin_ref[...]

    def proj(c0, c1):
        return _dot(xb, w_ref[:, c0:c1])

    def rope(p, scale):
        outs = []
        for h in range(p.shape[0] // HEAD_DIM):
            x1 = p[h * HEAD_DIM:h * HEAD_DIM + HALF]
            x2 = p[h * HEAD_DIM + HALF:(h + 1) * HEAD_DIM]
            outs.append((x1 * cos - x2 * sin) * scale)
            outs.append((x2 * cos + x1 * sin) * scale)
        return jnp.concatenate(outs, axis=0)

    def store_fm(ref, val):
        for c in range(tm // C):
            ref[0, c] = val[:, c * C:(c + 1) * C].astype(ref.dtype)

    qscale = math.log2(math.e) / math.sqrt(HEAD_DIM)
    store_fm(qa_ref, rope(proj(O_QA, O_KA).T, qscale))
    ka_ref[0] = rope(proj(O_KA, O_VA).T, 1.0).T.astype(ka_ref.dtype)
    store_fm(va_ref, proj(O_VA, O_QB).T)
    store_fm(qb_ref, proj(O_QB, O_KB).T * qscale)
    kb_ref[0] = proj(O_KB, O_VB).astype(kb_ref.dtype)
    store_fm(vb_ref, proj(O_VB, O_IDX).T)
    idx = proj(O_IDX, O_END).T
    store_fm(qi_ref, rope(idx[:W_IDX], 1.0))
    ki = idx[W_IDX:W_IDX + HEAD_DIM]
    mu = jnp.mean(ki, axis=0, keepdims=True)
    d = ki - mu
    var = jnp.mean(d * d, axis=0, keepdims=True)
    ki = rope(d * lax.rsqrt(var + LN_EPS) * g_ref[...] + b_ref[...], 1.0)
    ki_ref[0] = jnp.concatenate([ki, jnp.zeros_like(ki)], axis=0).T.astype(ki_ref.dtype)
    w_scale = IDX_HEADS ** -0.5 * HEAD_DIM ** -0.5
    store_fm(wi_ref, idx[W_IDX + HEAD_DIM:W_IDX + HEAD_DIM + SUBLANES] * w_scale)


def _inproj(x, w, cos_t, sin_t, g, b, *, tm):
    bsz, seq, d = x.shape
    nch = seq // C
    fm = lambda rows, dt: jax.ShapeDtypeStruct((bsz, nch, rows, C), dt)
    fm_spec = lambda rows: pl.BlockSpec((1, tm // C, rows, C), lambda bi, ti: (bi, ti, 0, 0))
    tok = lambda cols: jax.ShapeDtypeStruct((bsz, seq, cols), BF16)
    tok_spec = lambda cols: pl.BlockSpec((1, tm, cols), lambda bi, ti: (bi, ti, 0))
    full = lambda a: pl.BlockSpec(a.shape, lambda bi, ti: (0,) * a.ndim)
    return pl.pallas_call(
        functools.partial(_inproj_kernel, tm=tm),
        grid=(bsz, seq // tm),
        in_specs=[
            tok_spec(d), full(w),
            pl.BlockSpec((HALF, tm), lambda bi, ti: (0, ti)),
            pl.BlockSpec((HALF, tm), lambda bi, ti: (0, ti)),
            full(g), full(b),
        ],
        out_specs=[fm_spec(W_ATT), tok_spec(W_ATT), fm_spec(W_ATT), fm_spec(W_ATT), tok_spec(W_ATT), fm_spec(W_ATT),
                   fm_spec(W_IDX), tok_spec(LANES), fm_spec(SUBLANES)],
        out_shape=[fm(W_ATT, BF16), tok(W_ATT), fm(W_ATT, BF16), fm(W_ATT, BF16), tok(W_ATT), fm(W_ATT, BF16),
                   fm(W_IDX, BF16), tok(LANES), fm(SUBLANES, F32)],
        compiler_params=pltpu.CompilerParams(
            dimension_semantics=("parallel", "parallel"), vmem_limit_bytes=VMEM_LIMIT),
        name="inproj",
    )(x, w, cos_t, sin_t, g, b)


def _dsa_kernel(qi_ref, ki_ref, wi_ref, qa_ref, ka_ref, va_ref, t_ref, y_ref,
                key_s, bias_s, m_s, acc_s, *, n_sel):
    i = pl.program_id(1)
    nvis = i + 1
    krow = lax.broadcasted_iota(jnp.int32, (C, C), 0)
    qcol = lax.broadcasted_iota(jnp.int32, (C, C), 1)
    qlane = lax.broadcasted_iota(jnp.int32, (1, C), 1)

    qi_t = qi_ref[0, 0]
    qi_top = jnp.concatenate([qi_t[_head_rows(h)] for h in range(IDX_HEADS)], axis=1)
    qi_rhs = jnp.concatenate([qi_top, jnp.zeros_like(qi_top)], axis=0)
    wi = wi_ref[0, 0]

    def tile_tree(x, op):
        tiles = [x[r:r + SUBLANES] for r in range(0, C, SUBLANES)]
        while len(tiles) > 1:
            tiles = [op(tiles[t], tiles[t + 1]) for t in range(0, len(tiles), 2)]
        return tiles[0]

    def score_chunk(j, carry):
        lg = _dot(ki_ref[0, pl.ds(pl.multiple_of(j * C, C), C), :], qi_rhs)
        sc = jnp.maximum(lg[:, :C], 0.0) * wi[0:1]
        for h in range(1, IDX_HEADS):
            sc = sc + jnp.maximum(lg[:, h * C:(h + 1) * C], 0.0) * wi[h:h + 1]
        visible = (j - i) * C + krow <= qcol
        key_s[j] = jnp.where(visible, sc, -jnp.inf)
        return (jnp.minimum(carry[0], tile_tree(jnp.where(visible, sc, jnp.inf), jnp.minimum)),
                jnp.maximum(carry[1], tile_tree(jnp.where(visible, sc, -jnp.inf), jnp.maximum)))

    smin, smax = _chunk_pairs(nvis, score_chunk, (jnp.full((SUBLANES, C), jnp.inf, F32),
                                                  jnp.full((SUBLANES, C), -jnp.inf, F32)))
    smin = jnp.min(smin, axis=0, keepdims=True)
    smax = jnp.max(smax, axis=0, keepdims=True)

    def count(pred):
        def body(j, acc):
            return acc + tile_tree(pred(j * C, key_s[j]), jnp.add)
        acc = _chunk_pairs(nvis, body, jnp.zeros((SUBLANES, C), F32))
        return jnp.sum(acc, axis=0, keepdims=True)

    def code_to_float(code):
        mag = jnp.abs(code) + (FLT_MIN_BITS - 1)
        bits = jnp.where(code > 0, mag, jnp.where(code < 0, mag | INT_MIN, 0))
        return lax.bitcast_convert_type(bits, F32)

    def float_to_code(v):
        bits = lax.bitcast_convert_type(v, jnp.int32)
        mag = jnp.maximum((bits & INT_MAX) - (FLT_MIN_BITS - 1), 0)
        return jnp.where(bits < 0, -mag, mag)

    n_vis_q = (i * C + qlane + 1).astype(F32)
    few = n_vis_q <= n_sel
    lo0 = jnp.where(few, CODE_LOWEST, float_to_code(smin))
    hi0 = jnp.where(few, CODE_LOWEST + 1, float_to_code(smax) + 1)
    n_active0 = jnp.sum(jnp.where(hi0 - 1 > lo0, 1.0, 0.0))

    def bisect_cond(st):
        return st[4] > 0.5

    def bisect_step(st, by_value, probe=None):
        lo, hi, c_lo, c_hi = st[:4]
        mid_v = float_to_code(0.5 * code_to_float(lo) + 0.5 * code_to_float(hi))
        mid_c = (lo >> 1) + (hi >> 1) + (lo & hi & 1)
        mid = jnp.maximum(lo + 1, jnp.minimum(jnp.where(by_value, mid_v, mid_c), hi - 1))
        if probe is not None:
            mid = jnp.where(jnp.logical_and(lo < probe, probe < hi), probe, mid)
        mid_f = code_to_float(mid)
        c = count(lambda pos, k: jnp.where(k >= mid_f, 1.0, 0.0))
        ge = c >= n_sel
        exact = c == n_sel
        lo_n = jnp.where(ge, mid, lo)
        hi_n = jnp.where(exact, mid + 1, jnp.where(ge, hi, mid))
        return (lo_n, hi_n, jnp.where(ge, c, c_lo), jnp.where(ge, c_hi, c)) + tuple(st[4:])

    def bisect_key(st):
        rounds = st[5]
        for _ in range(BISECT_UNROLL):
            st = bisect_step(st, rounds < VALUE_BISECT_ROUNDS)
        lo, hi = st[0], st[1]
        return st[:4] + (jnp.sum(jnp.where(hi - 1 > lo, 1.0, 0.0)), rounds + 1)

    st0 = (lo0, hi0, n_vis_q, jnp.zeros((1, C), F32), n_active0, jnp.int32(0))
    st0 = bisect_step(bisect_step(st0, True, probe=0), True, probe=1)
    thr_code, _, c_thr, c_above, _, _ = lax.while_loop(bisect_cond, bisect_key, st0)
    thr = code_to_float(thr_code)

    need = jnp.where(c_thr > n_sel, n_sel - c_above, float(n_sel))

    def bias_chunk(j, ties_before):
        k = key_s[j]
        tie = jnp.where(k == thr, 1.0, 0.0).astype(BF16)
        sums = _dot(t_ref[...], tie)
        keep = jnp.where(ties_before + sums[:C] < need, 0.0, NEG_BIG)
        bias_s[j] = jnp.where(k > thr, 0.0, jnp.where(k == thr, keep, NEG_BIG))
        return ties_before + sums[C:C + 1]

    _chunk_pairs(nvis, bias_chunk, jnp.zeros((1, C), F32))

    q_blocks = _pair_queries(qa_ref[0, 0])
    m_s[...] = jnp.full(m_s.shape, NEG_BIG, F32)
    acc_s[...] = jnp.zeros(acc_s.shape, F32)
    ones_rows = jnp.ones((V_EXT - HEAD_DIM, C), BF16)

    def attend(j, carry):
        k_c = ka_ref[0, pl.ds(pl.multiple_of(j * C, C), C), :]
        s = jnp.concatenate([_dot(k_c[:, p * LANES:(p + 1) * LANES], q_blocks[p]) for p in range(N_PAIRS)], axis=1)
        s = s + jnp.concatenate([bias_s[j]] * N_HEADS, axis=1)
        m_old = m_s[...]
        m_new = jnp.maximum(m_old, _reduce_rows(s, jnp.maximum))
        alpha = jnp.exp2(m_old - m_new)
        pr = jnp.exp2(s - m_new).astype(BF16)
        m_s[...] = m_new
        for h in range(N_HEADS):
            v_ext = jnp.concatenate([va_ref[0, j, _head_rows(h), :], ones_rows], axis=0)
            acc_s[h] = acc_s[h] * alpha[:, h * C:(h + 1) * C] + _dot(v_ext, pr[:, h * C:(h + 1) * C])
        return carry

    _chunk_pairs(nvis, attend, 0)
    outs = []
    for h in range(N_HEADS):
        a = acc_s[h]
        outs.append(a[:HEAD_DIM] / a[HEAD_DIM:HEAD_DIM + 1])
    y_ref[0] = jnp.concatenate(outs, axis=0).T.astype(y_ref.dtype)


def _dsa(qi, ki, wi, qa, ka, va, t):
    bsz, nch, _, _ = qa.shape
    seq = nch * C
    n_sel = min(N_SEL_MAX, seq // 4)
    qspec = lambda rows: pl.BlockSpec((1, 1, rows, C), lambda bi, qi_: (bi, qi_, 0, 0))
    return pl.pallas_call(
        functools.partial(_dsa_kernel, n_sel=n_sel),
        grid=(bsz, nch),
        in_specs=[qspec(W_IDX),
                  pl.BlockSpec((1, seq, LANES), lambda bi, qi_: (bi, 0, 0)),
                  qspec(SUBLANES), qspec(W_ATT),
                  pl.BlockSpec((1, seq, W_ATT), lambda bi, qi_: (bi, 0, 0)),
                  pl.BlockSpec((1, nch, W_ATT, C), lambda bi, qi_: (bi, 0, 0, 0)),
                  pl.BlockSpec(t.shape, lambda bi, qi_: (0, 0))],
        out_specs=pl.BlockSpec((1, C, W_ATT), lambda bi, qi_: (bi, qi_, 0)),
        out_shape=jax.ShapeDtypeStruct((bsz, seq, W_ATT), BF16),
        scratch_shapes=[
            pltpu.VMEM((nch, C, C), F32),
            pltpu.VMEM((nch, C, C), F32),
            pltpu.VMEM((1, N_HEADS * C), F32),
            pltpu.VMEM((N_HEADS, V_EXT, C), F32),
        ],
        compiler_params=pltpu.CompilerParams(
            dimension_semantics=("parallel", "arbitrary"), vmem_limit_bytes=VMEM_LIMIT),
        name="dsa",
    )(qi, ki, wi, qa, ka, va, t)


def _sb_kernel(q_ref, k_ref, v_ref, u_ref, y_ref, acc_s, carry_s):
    i = pl.program_id(1)
    q_blocks = _pair_queries(q_ref[0, 0])
    acc_s[...] = jnp.zeros(acc_s.shape, F32)
    carry_s[...] = jnp.zeros(carry_s.shape, F32)

    def step(j, diagonal):
        k_c = k_ref[0, pl.ds(pl.multiple_of(j * C, C), C), :]
        z = jnp.concatenate([_dot(k_c[:, p * LANES:(p + 1) * LANES], q_blocks[p]) for p in range(N_PAIRS)], axis=1)
        sp = jnp.log2(1.0 + jnp.exp2(-jnp.abs(z)))
        log_beta = jnp.minimum(z, 0.0) - sp
        log_rest = log_beta - z
        if diagonal:
            before = (lax.broadcasted_iota(jnp.int32, z.shape, 0)
                      < (lax.broadcasted_iota(jnp.int32, z.shape, 1) & (C - 1)))
            log_rest = jnp.where(before, log_rest, 0.0)
        sums = _dot(u_ref[...], log_rest.astype(BF16))
        carry = carry_s[...]
        att = jnp.exp2(log_beta + (sums[:C] + carry))
        if diagonal:
            att = jnp.where(before, att, 0.0)
        carry_s[...] = carry + sums[C:C + 1]
        att = att.astype(BF16)
        for h in range(N_HEADS):
            acc_s[h] += _dot(v_ref[0, j, _head_rows(h), :], att[:, h * C:(h + 1) * C])

    step(i, True)

    def earlier(it, carry):
        step(i - 1 - it, False)
        return carry

    _chunk_pairs(i, earlier, 0)
    y_ref[0] = jnp.concatenate([acc_s[h] for h in range(N_HEADS)], axis=0).T.astype(y_ref.dtype)


def _sb(q, k, v, u):
    bsz, nch, _, _ = q.shape
    seq = nch * C
    return pl.pallas_call(
        _sb_kernel,
        grid=(bsz, nch),
        in_specs=[pl.BlockSpec((1, 1, W_ATT, C), lambda bi, qi_: (bi, qi_, 0, 0)),
                  pl.BlockSpec((1, seq, W_ATT), lambda bi, qi_: (bi, 0, 0)),
                  pl.BlockSpec((1, nch, W_ATT, C), lambda bi, qi_: (bi, 0, 0, 0)),
                  pl.BlockSpec(u.shape, lambda bi, qi_: (0, 0))],
        out_specs=pl.BlockSpec((1, C, W_ATT), lambda bi, qi_: (bi, qi_, 0)),
        out_shape=jax.ShapeDtypeStruct((bsz, seq, W_ATT), BF16),
        scratch_shapes=[pltpu.VMEM((N_HEADS, HEAD_DIM, C), F32),
                        pltpu.VMEM((1, N_HEADS * C), F32)],
        compiler_params=pltpu.CompilerParams(
            dimension_semantics=("parallel", "arbitrary"), vmem_limit_bytes=VMEM_LIMIT),
        name="stickbreak",
    )(q, k, v, u)


def _route(scores, biased):
    neg_inf = -jnp.inf
    group_score = []
    for g in range(N_GROUPS):
        v = biased[g * GROUP:(g + 1) * GROUP]
        best = None
        for a in range(GROUP):
            for b in range(a + 1, GROUP):
                pair = v[a] + v[b]
                best = pair if best is None else jnp.maximum(best, pair)
        group_score.append(best)
    gmax = functools.reduce(jnp.maximum, group_score)
    taken = jnp.zeros_like(gmax) > 1.0
    in_group = []
    for g in range(N_GROUPS):
        sel = jnp.logical_and(group_score[g] == gmax, jnp.logical_not(taken))
        taken = jnp.logical_or(taken, sel)
        in_group.append(sel)
    masked = [jnp.where(in_group[e // GROUP], biased[e], neg_inf) for e in range(N_EXPERTS)]

    def first_argmax(vals):
        vmax = functools.reduce(jnp.maximum, vals)
        taken_ = jnp.zeros_like(vmax) > 1.0
        picks = []
        for v in vals:
            sel = jnp.logical_and(v == vmax, jnp.logical_not(taken_))
            taken_ = jnp.logical_or(taken_, sel)
            picks.append(sel)
        return picks

    pick1 = first_argmax(masked)
    masked2 = [jnp.where(pick1[e], neg_inf, masked[e]) for e in range(N_EXPERTS)]
    pick2 = first_argmax(masked2)
    w1 = functools.reduce(jnp.add, [jnp.where(pick1[e], scores[e], 0.0) for e in range(N_EXPERTS)])
    w2 = functools.reduce(jnp.add, [jnp.where(pick2[e], scores[e], 0.0) for e in range(N_EXPERTS)])
    tot = w1 + w2
    return [jnp.where(pick1[e], w1 / tot, 0.0) + jnp.where(pick2[e], w2 / tot, 0.0)
            for e in range(N_EXPERTS)]


def _split_bf16(v):
    hi = v.astype(BF16)
    return hi, (v - hi.astype(F32)).astype(BF16)


def _merge_kernel(x_ref, ya_ref, yb_ref, wa_ref, wb_ref, wg_ref, bg_ref, wo_ref, g1_ref, b1_ref,
                  wr_hi_ref, wr_lo_ref, rb_ref, x1_ref, comb_ref, *, d):
    x = x_ref[...]
    gates = jax.nn.sigmoid(_dot(x.astype(BF16), wg_ref[...]) + bg_ref[...])
    a = _dot(ya_ref[...], wa_ref[...])
    b = _dot(yb_ref[...], wb_ref[...])
    merged = gates[:, :d] * a + gates[:, d:] * b
    mix = _dot(merged.astype(BF16), wo_ref[...])
    x1 = _layer_norm_rows(DN_ALPHA * x + mix, g1_ref[...], b1_ref[...])
    x1_ref[...] = x1
    x_hi, x_lo = _split_bf16(x1)
    nt = lambda w_, x_: lax.dot_general(w_, x_, (((1,), (1,)), ((), ())), preferred_element_type=F32)
    logits = nt(wr_hi_ref[...], x_hi) + (nt(wr_hi_ref[...], x_lo) + nt(wr_lo_ref[...], x_hi))
    sc = jax.nn.sigmoid(logits)
    bs = sc + rb_ref[...]
    scores = [sc[e:e + 1] for e in range(N_EXPERTS)]
    biased = [bs[e:e + 1] for e in range(N_EXPERTS)]
    comb_ref[...] = jnp.concatenate(_route(scores, biased), axis=0)


def _merge(x2d, ya, yb, wa, wb, wg, bg, wo, g1, b1, wr_hi, wr_lo, rb, *, tm):
    n, d = x2d.shape
    full = lambda a: pl.BlockSpec(a.shape, lambda ti: (0,) * a.ndim)
    row = lambda cols: pl.BlockSpec((tm, cols), lambda ti: (ti, 0))
    return pl.pallas_call(
        functools.partial(_merge_kernel, d=d),
        grid=(n // tm,),
        in_specs=[row(d), row(W_ATT), row(W_ATT), full(wa), full(wb), full(wg), full(bg), full(wo),
                  full(g1), full(b1), full(wr_hi), full(wr_lo), full(rb)],
        out_specs=[row(d), pl.BlockSpec((N_EXPERTS, tm), lambda ti: (0, ti))],
        out_shape=[jax.ShapeDtypeStruct((n, d), F32), jax.ShapeDtypeStruct((N_EXPERTS, n), F32)],
        compiler_params=pltpu.CompilerParams(
            dimension_semantics=("parallel",), vmem_limit_bytes=VMEM_LIMIT),
        name="merge",
    )(x2d, ya, yb, wa, wb, wg, bg, wo, g1, b1, wr_hi, wr_lo, rb)


def _moe_kernel(x_ref, comb_ref, combt_ref, l_ref, wg_ref, wu_ref, wd_ref, g2_ref, b2_ref, out_ref,
                acc_s, xb_s, rank_s, *, tm):
    e = pl.program_id(1)

    @pl.when(e == 0)
    def _():
        acc_s[...] = jnp.zeros(acc_s.shape, F32)
        xb_s[...] = x_ref[...].astype(BF16)
        picked = jnp.where(comb_ref[...] > 0.0, 1.0, 0.0).astype(BF16)
        rank_s[...] = _dot(picked, l_ref[...])

    lane = lax.broadcasted_iota(jnp.int32, combt_ref.shape, 1)
    slot_iota = lax.broadcasted_iota(jnp.int32, (MOE_SLOTS, tm), 0).astype(F32)
    pad_sel = jnp.zeros((MOE_SCATTER_K - MOE_SLOTS, tm), F32)
    pad_y = jnp.zeros((MOE_SCATTER_K - MOE_SLOTS, acc_s.shape[1]), BF16)

    for local in range(MOE_EXPERTS_PER_STEP):
        ex = e * MOE_EXPERTS_PER_STEP + local
        c_row = comb_ref[pl.ds(ex, 1), :]
        slot_of = jnp.where(c_row > 0.0, rank_s[pl.ds(ex, 1), :], -1.0)
        n_routed = jnp.sum(jnp.where(c_row > 0.0, 1.0, 0.0))
        n_blocks = ((n_routed + (MOE_SLOTS - 1)) * (1.0 / MOE_SLOTS)).astype(jnp.int32)
        c_col = jnp.sum(jnp.where(lane == ex, combt_ref[...], 0.0), axis=-1, keepdims=True)

        def block(b, carry, slot_of=slot_of, c_col=c_col, local=local):
            sel = jnp.where(slot_of == slot_iota + (b * MOE_SLOTS).astype(F32), 1.0, 0.0)
            xs = _dot(sel.astype(BF16), xb_s[...]).astype(BF16)
            h = jax.nn.silu(_dot(xs, wg_ref[local])) * _dot(xs, wu_ref[local])
            y = _dot(h.astype(BF16), wd_ref[local]).astype(BF16)
            sel_t = jnp.concatenate([sel, pad_sel], axis=0).T.astype(BF16)
            acc_s[...] += c_col * _dot(sel_t, jnp.concatenate([y, pad_y], axis=0))
            return carry

        lax.fori_loop(0, n_blocks, block, 0)

    @pl.when(e == pl.num_programs(1) - 1)
    def _():
        out_ref[...] = _layer_norm_rows(DN_ALPHA * x_ref[...] + acc_s[...], g2_ref[...], b2_ref[...])


def _moe(x1, comb, comb_t, later, wg, wu, wd, g2, b2, *, tm):
    n, d = x1.shape
    ne, _, de = wg.shape
    full = lambda a: pl.BlockSpec(a.shape, lambda ti, e: (0,) * a.ndim)
    return pl.pallas_call(
        functools.partial(_moe_kernel, tm=tm),
        grid=(n // tm, ne // MOE_EXPERTS_PER_STEP),
        in_specs=[pl.BlockSpec((tm, d), lambda ti, e: (ti, 0)),
                  pl.BlockSpec((ne, tm), lambda ti, e: (0, ti)),
                  pl.BlockSpec((tm, ne), lambda ti, e: (ti, 0)),
                  full(later),
                  pl.BlockSpec((MOE_EXPERTS_PER_STEP, d, de), lambda ti, e: (e, 0, 0)),
                  pl.BlockSpec((MOE_EXPERTS_PER_STEP, d, de), lambda ti, e: (e, 0, 0)),
                  pl.BlockSpec((MOE_EXPERTS_PER_STEP, de, d), lambda ti, e: (e, 0, 0)),
                  full(g2), full(b2)],
        out_specs=pl.BlockSpec((tm, d), lambda ti, e: (ti, 0)),
        out_shape=jax.ShapeDtypeStruct((n, d), F32),
        scratch_shapes=[pltpu.VMEM((tm, d), F32), pltpu.VMEM((tm, d), BF16), pltpu.VMEM((ne, tm), F32)],
        compiler_params=pltpu.CompilerParams(
            dimension_semantics=("parallel", "arbitrary"), vmem_limit_bytes=VMEM_LIMIT),
        name="moe",
    )(x1, comb, comb_t, later, wg, wu, wd, g2, b2)


def _rope_tables(seq):
    inv_freq = ROPE_THETA ** (-jnp.arange(HALF, dtype=F32) / HALF)
    ang = inv_freq[:, None] * jnp.arange(seq, dtype=F32)[None, :]
    return jnp.cos(ang), jnp.sin(ang)


def _later_key_matrix():
    s = jnp.arange(C)[:, None]
    j = jnp.arange(C)[None, :]
    return jnp.concatenate([(j > s).astype(BF16), jnp.ones((16, C), BF16)], axis=0)


def _earlier_key_matrix():
    s = jnp.arange(C)[:, None]
    j = jnp.arange(C)[None, :]
    return jnp.concatenate([(j < s).astype(BF16), jnp.ones((16, C), BF16)], axis=0)


def kernel(x, w_in, b_gate, idx_k_norm_g, idx_k_norm_b, w_branch_a, w_branch_b, w_out, ln1_g, ln1_b,
           w_router, router_bias, exp_w_gate, exp_w_up, exp_w_down, ln2_g, ln2_b):
    bsz, seq, d = x.shape
    n = bsz * seq
    cos_t, sin_t = _rope_tables(seq)
    u = _later_key_matrix()
    t = _earlier_key_matrix()
    tok = jnp.arange(MOE_TILE)
    moe_later = (tok[:, None] < tok[None, :]).astype(BF16)
    wr_hi, wr_lo = _split_bf16(w_router.T)
    rb = router_bias.reshape(N_EXPERTS, 1)
    o_qi = 3 * W_ATT
    o_qb = o_qi + W_IDX + HEAD_DIM + IDX_HEADS
    o_g = o_qb + 3 * W_ATT
    for l in range(DEPTH):
        w = w_in[l]
        wp = _cast_bf16(jnp.concatenate(
            [w[:, :o_qi], w[:, o_qb:o_g], w[:, o_qi:o_qb], jnp.zeros((d, O_END - o_g), F32)], axis=1), rows=256)
        wg = _cast_bf16(w[:, o_g:], rows=256)
        qa, ka, va, qb, kb, vb, qi, ki, wi = _inproj(
            x, wp, cos_t, sin_t, idx_k_norm_g[l].reshape(HEAD_DIM, 1), idx_k_norm_b[l].reshape(HEAD_DIM, 1), tm=512)
        ya = _dsa(qi, ki, wi, qa, ka, va, t)
        yb = _sb(qb, kb, vb, u)
        x1, comb = _merge(
            x.reshape(n, d), ya.reshape(n, W_ATT), yb.reshape(n, W_ATT),
            w_branch_a[l].astype(BF16), w_branch_b[l].astype(BF16), wg,
            b_gate[l].reshape(1, 2 * d), w_out[l].astype(BF16), ln1_g[l].reshape(1, d), ln1_b[l].reshape(1, d),
            wr_hi, wr_lo, rb, tm=512)
        x = _moe(x1, comb, comb.T, moe_later, exp_w_gate[l].astype(BF16), exp_w_up[l].astype(BF16),
                 exp_w_down[l].astype(BF16), ln2_g[l].reshape(1, d), ln2_b[l].reshape(1, d),
                 tm=MOE_TILE).reshape(bsz, seq, d)
    return x
```
